```python
import jax
import jax.numpy as jnp
from jax import lax
import numpy as np

D_MODEL = 1024
BATCH = 32
SEQ = 2048
DEPTH = 2

GRID_W = 64
CTX_LEN = 256
NORM_EPS = 1e-6

MLA_HEADS = 8
MLA_Q_RANK = 256
MLA_KV_RANK = 128
MLA_NOPE = 64
MLA_ROPE = 32
MLA_V = 64
MLA_WIDTH = MLA_HEADS * MLA_V
ROPE_BASE = 10000.0
Q_BLOCK = 128

POOL_WINDOWS = (2, 4, 8, 16)
POOL_WIDTH = 512
POOL_GROUP = POOL_WIDTH // len(POOL_WINDOWS)

GLA_HEADS = 4
GLA_DK = 64
GLA_DV = 128
GLA_KW = GLA_HEADS * GLA_DK
GLA_WIDTH = GLA_HEADS * GLA_DV
GLA_GATE_RANK = 16
GLA_TAU = 16.0
GLA_CHUNK = 64

N_BRANCH = 3

IN_NAMES = ('mla_q', 'mla_kv', 'mla_kr', 'mla_gate', 'pool_x', 'pool_gate',
            'gla_q', 'gla_k', 'gla_v', 'gla_af', 'gla_ab', 'gla_gate', 'merge')
IN_SIZES = (MLA_Q_RANK, MLA_KV_RANK, MLA_ROPE, MLA_WIDTH, POOL_WIDTH, POOL_WIDTH,
            GLA_KW, GLA_KW, GLA_WIDTH, GLA_GATE_RANK, GLA_GATE_RANK, GLA_WIDTH, N_BRANCH * D_MODEL)
D_IN = sum(IN_SIZES)

kernel_name = 'hybrid_mla_pool_gla_prefix_dit'


def rmsnorm(x, g):
    xf = x.astype(jnp.float32)
    y = xf * lax.rsqrt(jnp.mean(xf * xf, axis=-1, keepdims=True) + NORM_EPS)
    return (y * g.astype(jnp.float32)).astype(x.dtype)


def split_columns(z):
    offsets = [int(o) for o in np.cumsum(IN_SIZES)[:-1]]
    return dict(zip(IN_NAMES, jnp.split(z, offsets, axis=-1)))


def flip(t):
    return t[:, ::-1]


def axial_rope_tables(row, col):
    half = MLA_ROPE // 2
    inv = ROPE_BASE ** (-jnp.arange(0, half, 2, dtype=jnp.float32) / half)
    ang_r = row.astype(jnp.float32)[:, None] * inv
    ang_c = col.astype(jnp.float32)[:, None] * inv
    ang = jnp.concatenate([ang_r, ang_r, ang_c, ang_c], axis=-1)
    return jnp.cos(ang), jnp.sin(ang)


def apply_rope(x, cos, sin):
    half = MLA_ROPE // 2
    quarter = half // 2

    def rot(v):
        return jnp.concatenate([-v[..., quarter:], v[..., :quarter]], axis=-1)

    rotated = jnp.concatenate([rot(x[..., :half]), rot(x[..., half:])], axis=-1)
    return (x * cos + rotated * sin).astype(x.dtype)


def softmax_attention(q, k, v, scale):
    s = jnp.einsum('bqhd,bkhd->bhqk', q, k).astype(jnp.float32) * scale
    p = jax.nn.softmax(s, axis=-1).astype(v.dtype)
    return jnp.einsum('bhqk,bkhd->bqhd', p, v)


def blocked_attention(q, k, v, scale):
    B, L, H, dk = q.shape
    nblk = L // Q_BLOCK
    qb = q.reshape(B, nblk, Q_BLOCK, H, dk).transpose(1, 0, 2, 3, 4)
    ob = lax.map(lambda qq: softmax_attention(qq, k, v, scale), qb)
    return ob.transpose(1, 0, 2, 3, 4).reshape(B, L, H, v.shape[-1])


def mla_queries(zz, q_norm, w_uq):
    B, L, _ = zz['mla_q'].shape
    q = (rmsnorm(zz['mla_q'], q_norm) @ w_uq).reshape(B, L, MLA_HEADS, MLA_NOPE + MLA_ROPE)
    return q[..., :MLA_NOPE], q[..., MLA_NOPE:]


def mla_keys_values(zz, kv_norm, w_ukv):
    B, L, _ = zz['mla_kv'].shape
    kv = (rmsnorm(zz['mla_kv'], kv_norm) @ w_ukv).reshape(B, L, MLA_HEADS, MLA_NOPE + MLA_V)
    return kv[..., :MLA_NOPE], kv[..., MLA_NOPE:]


def assemble_keys(k_nope, k_rope):
    B, L, H, _ = k_nope.shape
    return jnp.concatenate([k_nope, jnp.broadcast_to(k_rope[:, :, None, :], (B, L, H, MLA_ROPE))], axis=-1)


def mla_branch(z, zc, cos, sin, q_norm, w_uq, kv_norm, w_ukv, with_ctx_out):
    B, L, _ = z['mla_q'].shape
    scale = (MLA_NOPE + MLA_ROPE) ** -0.5
    q_nope, q_rope = mla_queries(z, q_norm, w_uq)
    q = jnp.concatenate([q_nope, apply_rope(q_rope, cos[:, None, :], sin[:, None, :])], axis=-1)
    k_nope, v = mla_keys_values(z, kv_norm, w_ukv)
    k = assemble_keys(k_nope, apply_rope(z['mla_kr'], cos, sin))
    kc_nope, vc = mla_keys_values(zc, kv_norm, w_ukv)
    kc = assemble_keys(kc_nope, zc['mla_kr'])
    k_all = jnp.concatenate([kc, k], axis=1)
    v_all = jnp.concatenate([vc, v], axis=1)
    y = blocked_attention(q, k_all, v_all, scale).reshape(B, L, MLA_WIDTH)
    y = y * jax.nn.silu(z['mla_gate'])
    if with_ctx_out:
        Bc, Lc, _ = zc['mla_q'].shape
        qc = jnp.concatenate(mla_queries(zc, q_norm, w_uq), axis=-1)
        yc = softmax_attention(qc, kc, vc, scale).reshape(Bc, Lc, MLA_WIDTH) * jax.nn.silu(zc['mla_gate'])
    else:
        yc = None
    return y, yc


def multiscale_pool(u):
    B, L, W = u.shape
    uf = u.astype(jnp.float32)
    csum = jnp.concatenate([jnp.zeros((B, 1, W), jnp.float32), jnp.cumsum(uf, axis=1)], axis=1)
    t = jnp.arange(L)
    outs = []
    for g, w in enumerate(POOL_WINDOWS):
        lo = jnp.clip(t - w // 2, 0, L)
        hi = jnp.clip(t + w // 2, 0, L)
        cs = csum[..., g * POOL_GROUP:(g + 1) * POOL_GROUP]
        count = (hi - lo).astype(jnp.float32)[None, :, None]
        outs.append((jnp.take(cs, hi, axis=1) - jnp.take(cs, lo, axis=1)) / count)
    return (jnp.concatenate(outs, axis=-1) - uf).astype(u.dtype)


def pool_branch(zz, pool_w, pool_scale):
    B, L, _ = zz['pool_x'].shape
    pooled = multiscale_pool(zz['pool_x']).reshape(B, L, len(POOL_WINDOWS), POOL_GROUP)
    mixed = jnp.einsum('blgi,gio->blgo', pooled, pool_w).reshape(B, L, POOL_WIDTH)
    return mixed * pool_scale * jax.nn.silu(zz['pool_gate'])


def gla_scan(q, k, v, log_a, s0, with_out):
    B, L, H, _ = q.shape
    n = L // GLA_CHUNK

    def to_chunks(t):
        return t.reshape(B, n, GLA_CHUNK, H, t.shape[-1]).transpose(1, 0, 3, 2, 4)

    mask = jnp.tril(jnp.ones((GLA_CHUNK, GLA_CHUNK), bool))[:, :, None]

    def step(s, inp):
        qq, kk, vv, aa = inp
        b = jnp.cumsum(aa, axis=2)
        b_last = b[:, :, -1:, :]
        s_new = jnp.exp(b_last)[:, :, 0, :, None] * s + jnp.einsum('bhcd,bhce->bhde', kk * jnp.exp(b_last - b), vv)
        if not with_out:
            return s_new, None
        inter = jnp.einsum('bhcd,bhde->bhce', qq * jnp.exp(b), s)
        decay = jnp.exp(jnp.where(mask, b[:, :, :, None, :] - b[:, :, None, :, :], -jnp.inf))
        attn = jnp.einsum('bhid,bhjd,bhijd->bhij', qq, kk, decay)
        intra = jnp.einsum('bhij,bhje->bhie', attn, vv)
        return s_new, inter + intra

    s_fin, out = lax.scan(step, s0, (to_chunks(q), to_chunks(k), to_chunks(v), to_chunks(log_a)))
    if with_out:
        out = out.transpose(1, 0, 3, 2, 4).reshape(B, L, H, v.shape[-1])
    return s_fin, out


def gla_inputs(zz, af_w2, af_b, ab_w2, ab_b):
    B, L, _ = zz['gla_v'].shape
    f32 = jnp.float32
    q = zz['gla_q'].astype(f32).reshape(B, L, GLA_HEADS, GLA_DK) * GLA_DK ** -0.5
    k = zz['gla_k'].astype(f32).reshape(B, L, GLA_HEADS, GLA_DK)
    v = zz['gla_v'].astype(f32).reshape(B, L, GLA_HEADS, GLA_DV)
    log_a_f = (jax.nn.log_sigmoid((zz['gla_af'] @ af_w2 + af_b).astype(f32)) / GLA_TAU).reshape(B, L, GLA_HEADS, GLA_DK)
    log_a_b = (jax.nn.log_sigmoid((zz['gla_ab'] @ ab_w2 + ab_b).astype(f32)) / GLA_TAU).reshape(B, L, GLA_HEADS, GLA_DK)
    return q, k, v, log_a_f, log_a_b


def gla_output(o, zz, g):
    B, L = o.shape[:2]
    o = rmsnorm(o, g).reshape(B, L, GLA_WIDTH).astype(zz['gla_gate'].dtype)
    return o * jax.nn.silu(zz['gla_gate'])


def gla_branch(z, zc, af_w2, af_b, ab_w2, ab_b, gla_norm, with_ctx_out):
    qc, kc, vc, afc, abc = gla_inputs(zc, af_w2, af_b, ab_w2, ab_b)
    s0 = jnp.zeros((qc.shape[0], GLA_HEADS, GLA_DK, GLA_DV), jnp.float32)
    sc_f, oc_f = gla_scan(qc, kc, vc, afc, s0, with_ctx_out)
    sc_b, oc_b = gla_scan(flip(qc), flip(kc), flip(vc), flip(abc), s0, with_ctx_out)
    q, k, v, af, ab = gla_inputs(z, af_w2, af_b, ab_w2, ab_b)
    _, o_f = gla_scan(q, k, v, af, sc_f, True)
    _, o_b = gla_scan(flip(q), flip(k), flip(v), flip(ab), sc_b, True)
    y = gla_output(o_f + flip(o_b), z, gla_norm)
    yc = gla_output(oc_f + flip(oc_b), zc, gla_norm) if with_ctx_out else None
    return y, yc


def merge_branches(zz, y_mla, y_pool, y_gla, w_bm, w_bp, w_bg, w_out):
    gates = jax.nn.sigmoid(zz['merge'].astype(jnp.float32)).astype(y_mla.dtype)
    g_mla, g_pool, g_gla = jnp.split(gates, N_BRANCH, axis=-1)
    merged = g_mla * (y_mla @ w_bm) + g_pool * (y_pool @ w_bp) + g_gla * (y_gla @ w_bg)
    return merged @ w_out


def trunk_layer(x, xc, mod, mod_c, cos, sin, pre_g, post_g, w_in, mla_q_norm, mla_w_uq, mla_kv_norm,
                mla_w_ukv, pool_w, pool_scale, gla_af_w2, gla_af_b, gla_ab_w2, gla_ab_b, gla_norm,
                w_branch_mla, w_branch_pool, w_branch_gla, w_out, with_ctx_out):
    shift, scale, gate = jnp.split(mod[:, None, :], 3, axis=-1)
    shift_c, scale_c, gate_c = jnp.split(mod_c[None, None, :], 3, axis=-1)
    z = split_columns((rmsnorm(x, pre_g) * (1 + scale) + shift) @ w_in)
    zc = split_columns((rmsnorm(xc, pre_g) * (1 + scale_c) + shift_c) @ w_in)
    y_mla, yc_mla = mla_branch(z, zc, cos, sin, mla_q_norm, mla_w_uq, mla_kv_norm, mla_w_ukv, with_ctx_out)
    y_pool = pool_branch(z, pool_w, pool_scale)
    y_gla, yc_gla = gla_branch(z, zc, gla_af_w2, gla_af_b, gla_ab_w2, gla_ab_b, gla_norm, with_ctx_out)
    out = merge_branches(z, y_mla, y_pool, y_gla, w_branch_mla, w_branch_pool, w_branch_gla, w_out)
    x = x + gate * rmsnorm(out, post_g)
    if with_ctx_out:
        yc_pool = pool_branch(zc, pool_w, pool_scale)
        out_c = merge_branches(zc, yc_mla, yc_pool, yc_gla, w_branch_mla, w_branch_pool, w_branch_gla, w_out)
        xc = xc + gate_c * rmsnorm(out_c, post_g)
    return x, xc


def setup_inputs(seed: int = 0) -> dict:
    key = jax.random.key(seed)
    ks = jax.random.split(key, 24)
    f32 = jnp.float32

    def nrm(k, shape, s):
        return jax.random.normal(k, shape, f32) * s

    def gain(k, n):
        return 1.0 + 0.1 * jax.random.normal(k, (DEPTH, n), f32)

    return {
        'x': nrm(ks[0], (BATCH, SEQ, D_MODEL), 1.0),
        'c': nrm(ks[1], (BATCH, D_MODEL), 1.0),
        'ctx': nrm(ks[2], (BATCH, CTX_LEN, D_MODEL), 1.0),
        'c_ctx': nrm(ks[3], (D_MODEL,), 1.0),
        'mod_w': nrm(ks[4], (DEPTH, D_MODEL, 3 * D_MODEL), 0.5 * D_MODEL ** -0.5),
        'mod_b': nrm(ks[5], (DEPTH, 3 * D_MODEL), 0.02),
        'pre_norm': gain(ks[6], D_MODEL),
        'post_norm': gain(ks[7], D_MODEL),
        'w_in': nrm(ks[8], (DEPTH, D_MODEL, D_IN), D_MODEL ** -0.5),
        'mla_q_norm': gain(ks[9], MLA_Q_RANK),
        'mla_w_uq': nrm(ks[10], (DEPTH, MLA_Q_RANK, MLA_HEADS * (MLA_NOPE + MLA_ROPE)), MLA_Q_RANK ** -0.5),
        'mla_kv_norm': gain(ks[11], MLA_KV_RANK),
        'mla_w_ukv': nrm(ks[12], (DEPTH, MLA_KV_RANK, MLA_HEADS * (MLA_NOPE + MLA_V)), MLA_KV_RANK ** -0.5),
        'pool_w': nrm(ks[13], (DEPTH, len(POOL_WINDOWS), POOL_GROUP, POOL_GROUP), POOL_GROUP ** -0.5),
        'pool_scale': gain(ks[14], POOL_WIDTH),
        'gla_af_w2': nrm(ks[15], (DEPTH, GLA_GATE_RANK, GLA_KW), GLA_GATE_RANK ** -0.5),
        'gla_af_b': nrm(ks[16], (DEPTH, GLA_KW), 0.1),
        'gla_ab_w2': nrm(ks[17], (DEPTH, GLA_GATE_RANK, GLA_KW), GLA_GATE_RANK ** -0.5),
        'gla_ab_b': nrm(ks[18], (DEPTH, GLA_KW), 0.1),
        'gla_norm': gain(ks[19], GLA_DV),
        'w_branch_mla': nrm(ks[20], (DEPTH, MLA_WIDTH, D_MODEL), MLA_WIDTH ** -0.5),
        'w_branch_pool': nrm(ks[21], (DEPTH, POOL_WIDTH, D_MODEL), POOL_WIDTH ** -0.5),
        'w_branch_gla': nrm(ks[22], (DEPTH, GLA_WIDTH, D_MODEL), GLA_WIDTH ** -0.5),
        'w_out': nrm(ks[23], (DEPTH, D_MODEL, D_MODEL), D_MODEL ** -0.5),
    }


def reference(x, c, ctx, c_ctx, mod_w, mod_b, pre_norm, post_norm, w_in, mla_q_norm, mla_w_uq,
              mla_kv_norm, mla_w_ukv, pool_w, pool_scale, gla_af_w2, gla_af_b, gla_ab_w2, gla_ab_b,
              gla_norm, w_branch_mla, w_branch_pool, w_branch_gla, w_out):
    n_tok = x.shape[1]
    rows = n_tok // GRID_W
    row = jnp.repeat(jnp.arange(rows), GRID_W)
    col = jnp.tile(jnp.arange(GRID_W), rows)
    cos, sin = axial_rope_tables(row, col)
    silu_c = jax.nn.silu(c)
    silu_cc = jax.nn.silu(c_ctx)
    xc = ctx
    for l in range(DEPTH):
        mod = silu_c @ mod_w[l] + mod_b[l]
        mod_c = silu_cc @ mod_w[l] + mod_b[l]
        x, xc = trunk_layer(x, xc, mod, mod_c, cos, sin, pre_norm[l], post_norm[l], w_in[l],
                            mla_q_norm[l], mla_w_uq[l], mla_kv_norm[l], mla_w_ukv[l], pool_w[l],
                            pool_scale[l], gla_af_w2[l], gla_af_b[l], gla_ab_w2[l], gla_ab_b[l],
                            gla_norm[l], w_branch_mla[l], w_branch_pool[l], w_branch_gla[l], w_out[l],
                            l < DEPTH - 1)
    return x
```

```python
import functools

import jax
import jax.numpy as jnp
from jax import lax
from jax.experimental import pallas as pl
from jax.experimental.pallas import tpu as pltpu

F32 = jnp.float32
BF16 = jnp.bfloat16

D_MODEL = 1024
NORM_EPS = 1e-6
GRID_W = 64
ROPE_BASE = 10000.0

MLA_HEADS = 8
MLA_Q_RANK = 256
MLA_KV_RANK = 128
MLA_NOPE = 64
MLA_ROPE = 32
MLA_V = 64
MLA_WIDTH = MLA_HEADS * MLA_V
HEAD_PAD = 128

POOL_WINDOWS = (2, 4, 8, 16)
POOL_WIDTH = 512
POOL_GROUP = 128
POOL_HALO = 16

GLA_HEADS = 4
GLA_DK = 64
GLA_DV = 128
GLA_KW = GLA_HEADS * GLA_DK
GLA_WIDTH = GLA_HEADS * GLA_DV
GLA_GATE_RANK = 16
GLA_TAU = 16.0
GLA_CHUNK = 128

N_BRANCH = 3
IN_SIZES = (MLA_Q_RANK, MLA_KV_RANK, MLA_ROPE, MLA_WIDTH, POOL_WIDTH, POOL_WIDTH,
            GLA_KW, GLA_KW, GLA_WIDTH, GLA_GATE_RANK, GLA_GATE_RANK, GLA_WIDTH, N_BRANCH * D_MODEL)

C_Q = 0
C_KV = C_Q + MLA_Q_RANK
C_KR = C_KV + MLA_KV_RANK
C_GLA = C_KR + 2 * HEAD_PAD
GLA_IN_W = 2 * GLA_KW + GLA_WIDTH + 128
C_POOL = C_GLA + GLA_IN_W
C_GATES = C_POOL + POOL_WIDTH
GATES_W = 3 * 512 + N_BRANCH * D_MODEL
C_END = C_GATES + GATES_W

TOKEN_TILE = 256
VMEM_LIMIT = 56 * 1024 * 1024


def _sigmoid(x):
    return 1.0 / (1.0 + jnp.exp(-x))


def _silu(x):
    return x * _sigmoid(x)


def _rmsnorm(x, g):
    return x * lax.rsqrt(jnp.mean(x * x, axis=-1, keepdims=True) + NORM_EPS) * g


def _dot(a, b):
    return jnp.dot(a, b, preferred_element_type=F32)


def _dot_nt(a, b):
    return lax.dot_general(a, b, (((1,), (1,)), ((), ())), preferred_element_type=F32)


def _dot_tn(a, b):
    return lax.dot_general(a, b, (((0,), (0,)), ((), ())), preferred_element_type=F32)


def _mod_kernel(c_ref, w_ref, b_ref, o_ref):
    s = _silu(c_ref[...]).astype(BF16)
    o_ref[0] = _dot(s, w_ref[0]) + b_ref[0]


def _modulation(c_all, mod_w, mod_b):
    depth, d, n = mod_w.shape
    mp = c_all.shape[0]
    tn = 1024
    return pl.pallas_call(
        _mod_kernel,
        grid=(depth, n // tn),
        in_specs=[pl.BlockSpec((mp, d), lambda l, j: (0, 0)),
                  pl.BlockSpec((1, d, tn), lambda l, j: (l, 0, j)),
                  pl.BlockSpec((1, 1, tn), lambda l, j: (l, 0, j))],
        out_specs=pl.BlockSpec((1, mp, tn), lambda l, j: (l, 0, j)),
        out_shape=jax.ShapeDtypeStruct((depth, mp, n), F32),
        name="modulation",
    )(c_all, mod_w, mod_b.reshape(depth, 1, n))


def _inproj_kernel(x_ref, ss_ref, pre_ref, w_ref, qn_ref, wuq_ref, kvn_ref, wukv_ref,
                   cq_ref, sq_ref, ck_ref, sk_ref,
                   q_ref, k_ref, v_ref, gla_ref, pool_ref, gates_ref):
    x = x_ref[...]
    ss = ss_ref[0]
    xn = _rmsnorm(x, pre_ref[...]) * (1.0 + ss[1:2]) + ss[0:1]
    xb = xn.astype(BF16)

    gla_ref[...] = _dot(xb, w_ref[:, C_GLA:C_POOL]).astype(gla_ref.dtype)
    pool_ref[...] = _dot(xb, w_ref[:, C_POOL:C_GATES]).astype(pool_ref.dtype)
    gate_chunk = GATES_W // 4
    for j in range(4):
        lo = j * gate_chunk
        gates_ref[:, lo:lo + gate_chunk] = _dot(
            xb, w_ref[:, C_GATES + lo:C_GATES + lo + gate_chunk]).astype(gates_ref.dtype)

    hw = MLA_HEADS * HEAD_PAD
    zq = _dot(xb, w_ref[:, C_Q:C_KV])
    qq = _dot(_rmsnorm(zq, qn_ref[...]).astype(BF16), wuq_ref[...])
    cq, sq = cq_ref[...], sq_ref[...]
    for h in range(MLA_HEADS):
        lo = h * HEAD_PAD
        q_ref[0, h] = (qq[:, lo:lo + HEAD_PAD] * cq + qq[:, hw + lo:hw + lo + HEAD_PAD] * sq).astype(q_ref.dtype)

    zkv = _dot(xb, w_ref[:, C_KV:C_KR])
    zkr = _dot(xb, w_ref[:, C_KR:C_GLA])
    k_rope = zkr[:, :HEAD_PAD] * ck_ref[...] + zkr[:, HEAD_PAD:] * sk_ref[...]
    kv = _dot(_rmsnorm(zkv, kvn_ref[...]).astype(BF16), wukv_ref[...])
    for h in range(MLA_HEADS):
        lo = h * HEAD_PAD
        k_ref[0, h] = (kv[:, lo:lo + HEAD_PAD] + k_rope).astype(k_ref.dtype)
        v_ref[0, h] = kv[:, hw + lo:hw + lo + HEAD_PAD].astype(v_ref.dtype)


def _inproj(x_all, ss, pre_g, w_in_r, q_norm, w_uq2, kv_norm, w_ukv2, tables, batch, l_all, ctx_tiles):
    n_tok = x_all.shape[0]
    tm = TOKEN_TILE
    tpb = l_all // tm
    cq, sq, ck, sk = tables
    hw = MLA_HEADS * HEAD_PAD

    def sel(i):
        return ((i // tpb) * 2 + jnp.where(i % tpb >= ctx_tiles, 1, 0), 0, 0)

    const = lambda i: (0, 0)
    tab = pl.BlockSpec((tm, HEAD_PAD), lambda i: (i % tpb, 0))
    head_spec = pl.BlockSpec((1, MLA_HEADS, tm, HEAD_PAD), lambda i: (i // tpb, 0, i % tpb, 0))
    head_shape = jax.ShapeDtypeStruct((batch, MLA_HEADS, l_all, HEAD_PAD), BF16)
    return pl.pallas_call(
        _inproj_kernel,
        grid=(n_tok // tm,),
        in_specs=[pl.BlockSpec((tm, D_MODEL), lambda i: (i, 0)),
                  pl.BlockSpec((1, 3, D_MODEL), sel),
                  pl.BlockSpec((1, D_MODEL), const),
                  pl.BlockSpec((D_MODEL, C_END), const, pipeline_mode=pl.Buffered(1)),
                  pl.BlockSpec((1, MLA_Q_RANK), const),
                  pl.BlockSpec((MLA_Q_RANK, 2 * hw), const),
                  pl.BlockSpec((1, MLA_KV_RANK), const),
                  pl.BlockSpec((MLA_KV_RANK, 2 * hw), const),
                  tab, tab, tab, tab],
        out_specs=[head_spec, head_spec, head_spec,
                   pl.BlockSpec((tm, GLA_IN_W), lambda i: (i, 0)),
                   pl.BlockSpec((tm, POOL_WIDTH), lambda i: (i, 0)),
                   pl.BlockSpec((tm, GATES_W), lambda i: (i, 0))],
        out_shape=[head_shape, head_shape, head_shape,
                   jax.ShapeDtypeStruct((n_tok, GLA_IN_W), BF16),
                   jax.ShapeDtypeStruct((n_tok, POOL_WIDTH), BF16),
                   jax.ShapeDtypeStruct((n_tok, GATES_W), BF16)],
        compiler_params=pltpu.CompilerParams(dimension_semantics=("parallel",),
                                             vmem_limit_bytes=VMEM_LIMIT),
        name="inproj",
    )(x_all, ss, pre_g, w_in_r, q_norm, w_uq2, kv_norm, w_ukv2, cq, sq, ck, sk)


def _attn_kernel(q_ref, k_ref, v_ref, o_ref, *, ctx_tiles, ctx_len, l_all):
    j = pl.program_id(1)

    def attend(nk):
        for p in range(MLA_HEADS // 2):
            acc = None
            for hh in range(2):
                h = 2 * p + hh
                s = _dot_nt(q_ref[0, h], k_ref[0, h, 0:nk, :])
                m = jnp.max(s, axis=-1, keepdims=True)
                e = jnp.exp2(s - m)
                l = jnp.sum(e, axis=-1, keepdims=True)
                pv = _dot(e.astype(BF16), v_ref[0, h, 0:nk, :])
                t = pv * (1.0 / l)
                acc = t if acc is None else acc + t
            o_ref[:, p * HEAD_PAD:(p + 1) * HEAD_PAD] = acc.astype(o_ref.dtype)

    @pl.when(j < ctx_tiles)
    def _():
        attend(ctx_len)

    @pl.when(j >= ctx_tiles)
    def _():
        attend(l_all)


def _attention(q, k, v, ctx_len):
    batch, heads, l_all, hp = q.shape
    tq = TOKEN_TILE
    tpb = l_all // tq
    kv_spec = pl.BlockSpec((1, heads, l_all, hp), lambda b, j: (b, 0, 0, 0))
    return pl.pallas_call(
        functools.partial(_attn_kernel, ctx_tiles=ctx_len // tq, ctx_len=ctx_len, l_all=l_all),
        grid=(batch, tpb),
        in_specs=[pl.BlockSpec((1, heads, tq, hp), lambda b, j: (b, 0, j, 0)), kv_spec, kv_spec],
        out_specs=pl.BlockSpec((tq, MLA_WIDTH), lambda b, j: (b * tpb + j, 0)),
        out_shape=jax.ShapeDtypeStruct((batch * l_all, MLA_WIDTH), BF16),
        compiler_params=pltpu.CompilerParams(dimension_semantics=("parallel", "arbitrary"),
                                             vmem_limit_bytes=VMEM_LIMIT),
        name="attention",
    )(q, k, v)


def _gla_direction(g, wg_ref, bg_ref, st_ref, o_ref, reverse):
    c = g.shape[0]
    q = g[:, 0:GLA_KW].astype(F32) * (GLA_DK ** -0.5)
    k = g[:, GLA_KW:2 * GLA_KW].astype(F32)
    v = g[:, 2 * GLA_KW:2 * GLA_KW + GLA_WIDTH]
    pre = _dot(g[:, 2 * GLA_KW + GLA_WIDTH:], wg_ref[...]) + bg_ref[...]
    log_a = (jnp.minimum(pre, 0.0) - jnp.log1p(jnp.exp(-jnp.abs(pre)))) * (1.0 / GLA_TAU)

    row = lax.broadcasted_iota(jnp.int32, (c, c), 0)
    col = lax.broadcasted_iota(jnp.int32, (c, c), 1)
    tri = (col >= row) if reverse else (col <= row)
    ones_tri = jnp.where(tri, 1.0, 0.0).astype(BF16)
    hi = log_a.astype(BF16)
    r1 = log_a - hi.astype(F32)
    mid = r1.astype(BF16)
    lo = (r1 - mid.astype(F32)).astype(BF16)
    b = _dot(ones_tri, hi) + _dot(ones_tri, mid) + _dot(ones_tri, lo)
    b_tot = b[0:1] if reverse else b[c - 1:c]

    q_inter = q * jnp.exp(b)
    q_intra = q * jnp.exp(b - b_tot)
    k_t = (k * jnp.exp(b_tot - b)).astype(BF16)
    dec = jnp.exp(b_tot)

    first = lax.broadcasted_iota(jnp.int32, (c, 2 * GLA_DK), 1) < GLA_DK
    first_st = lax.broadcasted_iota(jnp.int32, (GLA_DV, 2 * GLA_DK), 1) < GLA_DK
    for p in range(GLA_HEADS // 2):
        sl = slice(p * 2 * GLA_DK, (p + 1) * 2 * GLA_DK)
        st = st_ref[p]
        st_b = st.astype(BF16)
        kt = k_t[:, sl]
        upd = []
        for hh in range(2):
            h = 2 * p + hh
            mask = first if hh == 0 else jnp.logical_not(first)
            qi = jnp.where(mask, q_intra[:, sl], 0.0).astype(BF16)
            qe = jnp.where(mask, q_inter[:, sl], 0.0).astype(BF16)
            vh = v[:, h * GLA_DV:(h + 1) * GLA_DV]
            a = jnp.where(tri, _dot_nt(qi, kt), 0.0).astype(BF16)
            o_ref[:, h * GLA_DV:(h + 1) * GLA_DV] = (_dot_nt(qe, st_b) + _dot(a, vh)).astype(o_ref.dtype)
            upd.append(_dot_tn(vh, kt))
        st_ref[p] = st * dec[:, sl] + jnp.where(first_st, upd[0], upd[1])


def _gla_kernel(gf_ref, gb_ref, wf_ref, bf_ref, wb_ref, bb_ref, of_ref, ob_ref, sf_ref, sb_ref):
    @pl.when(pl.program_id(1) == 0)
    def _():
        sf_ref[...] = jnp.zeros_like(sf_ref)
        sb_ref[...] = jnp.zeros_like(sb_ref)

    _gla_direction(gf_ref[...], wf_ref, bf_ref, sf_ref, of_ref, False)
    _gla_direction(gb_ref[...], wb_ref, bb_ref, sb_ref, ob_ref, True)


def _gla(gla_in, wf, bf, wb, bb, batch, l_all, ctx_len):
    c = GLA_CHUNK
    nch = l_all // c
    ncc = ctx_len // c

    def bwd_chunk(s):
        return jnp.where(s < ncc, ncc - 1 - s, nch - 1 + ncc - s)

    const = lambda b, s: (0, 0)
    wspec = pl.BlockSpec((128, GLA_KW), const)
    bspec = pl.BlockSpec((1, GLA_KW), const)
    fwd = lambda b, s: (b * nch + s, 0)
    bwd = lambda b, s: (b * nch + bwd_chunk(s), 0)
    out_shape = jax.ShapeDtypeStruct((batch * l_all, GLA_WIDTH), F32)
    return pl.pallas_call(
        _gla_kernel,
        grid=(batch, nch),
        in_specs=[pl.BlockSpec((c, GLA_IN_W), fwd), pl.BlockSpec((c, GLA_IN_W), bwd),
                  wspec, bspec, wspec, bspec],
        out_specs=[pl.BlockSpec((c, GLA_WIDTH), fwd), pl.BlockSpec((c, GLA_WIDTH), bwd)],
        out_shape=[out_shape, out_shape],
        scratch_shapes=[pltpu.VMEM((GLA_HEADS // 2, GLA_DV, 2 * GLA_DK), F32),
                        pltpu.VMEM((GLA_HEADS // 2, GLA_DV, 2 * GLA_DK), F32)],
        compiler_params=pltpu.CompilerParams(dimension_semantics=("parallel", "arbitrary")),
        name="gla_scan",
    )(gla_in, gla_in, wf, bf, wb, bb)


def _merge_kernel(att_ref, gates_ref, pc_ref, pp_ref, pn_ref, of_ref, ob_ref, x_ref, ss_ref,
                  gn_ref, post_ref, pw_ref, ps_ref, wbm_ref, wbp_ref, wbg_ref, wout_ref,
                  o_ref, ubuf, *, tpb, ctx_tiles, ctx_len, lat_len, latent_only):
    tm = x_ref.shape[0]
    i = pl.program_id(0)
    if latent_only:
        jt = i % (tpb - ctx_tiles) + ctx_tiles
    else:
        jt = i % tpb
    is_ctx = jt < ctx_tiles
    p0 = jnp.where(is_ctx, jt, jt - ctx_tiles) * tm
    seq_len = jnp.where(is_ctx, ctx_len, lat_len)

    hal = POOL_HALO
    ubuf[0:hal] = jnp.where(p0 > 0, pp_ref[...].astype(F32), 0.0)
    ubuf[hal:hal + tm] = pc_ref[...].astype(F32)
    ubuf[hal + tm:hal + tm + hal] = jnp.where(p0 + tm < seq_len, pn_ref[...].astype(F32), 0.0)
    tpos = p0 + lax.broadcasted_iota(jnp.int32, (tm, POOL_GROUP), 0)
    pooled = []
    for g, w in enumerate(POOL_WINDOWS):
        lanes = slice(g * POOL_GROUP, (g + 1) * POOL_GROUP)
        acc = None
        for d in range(-(w // 2), w // 2):
            piece = ubuf[hal + d:hal + d + tm, lanes]
            acc = piece if acc is None else acc + piece
        count = (jnp.minimum(tpos + w // 2, seq_len) - jnp.maximum(tpos - w // 2, 0)).astype(F32)
        pooled.append((acc / count - ubuf[hal:hal + tm, lanes]).astype(BF16))
    mixed = jnp.concatenate(
        [_dot(jnp.concatenate(pooled[2 * j:2 * j + 2], axis=-1), pw_ref[j]) for j in range(2)], axis=-1)

    y_mla = att_ref[...].astype(F32) * _silu(gates_ref[:, 0:512].astype(F32))
    y_pool = mixed * ps_ref[...] * _silu(gates_ref[:, 512:1024].astype(F32))
    o = of_ref[...] + ob_ref[...]
    gn = gn_ref[...]
    o = jnp.concatenate([_rmsnorm(o[:, h * GLA_DV:(h + 1) * GLA_DV], gn) for h in range(GLA_HEADS)], axis=-1)
    y_gla = o * _silu(gates_ref[:, 1024:1536].astype(F32))

    def gate(j):
        lo = 1536 + j * D_MODEL
        return _sigmoid(gates_ref[:, lo:lo + D_MODEL].astype(F32))

    merged = gate(0) * _dot(y_mla.astype(BF16), wbm_ref[...])
    merged = merged + gate(1) * _dot(y_pool.astype(BF16), wbp_ref[...])
    merged = merged + gate(2) * _dot(y_gla.astype(BF16), wbg_ref[...])
    out = _dot(merged.astype(BF16), wout_ref[...])
    o_ref[...] = x_ref[...] + ss_ref[0][2:3] * _rmsnorm(out, post_ref[...])


def _merge(att, gates, pool_x, o_f, o_b, x_all, ss, gla_norm, post_g, pool_w2, pool_scale,
           w_bm, w_bp, w_bg, w_out, batch, l_all, ctx_len, latent_only):
    tm = TOKEN_TILE
    tpb = l_all // tm
    ctx_tiles = ctx_len // tm
    lat_tiles = tpb - ctx_tiles
    hb = tm // POOL_HALO
    n_halo_blocks = pool_x.shape[0] // POOL_HALO

    if latent_only:
        def tile(i):
            return (i // lat_tiles) * tpb + ctx_tiles + i % lat_tiles
        n_out_tiles = batch * lat_tiles
    else:
        def tile(i):
            return i
        n_out_tiles = batch * tpb

    def sel(i):
        t = tile(i)
        return ((t // tpb) * 2 + jnp.where(t % tpb >= ctx_tiles, 1, 0), 0, 0)

    row = lambda i: (tile(i), 0)
    const2 = lambda i: (0, 0)
    const3 = lambda i: (0, 0, 0)
    return pl.pallas_call(
        functools.partial(_merge_kernel, tpb=tpb, ctx_tiles=ctx_tiles, ctx_len=ctx_len,
                          lat_len=l_all - ctx_len, latent_only=latent_only),
        grid=(n_out_tiles,),
        in_specs=[pl.BlockSpec((tm, MLA_WIDTH), row),
                  pl.BlockSpec((tm, GATES_W), row),
                  pl.BlockSpec((tm, POOL_WIDTH), row),
                  pl.BlockSpec((POOL_HALO, POOL_WIDTH), lambda i: (jnp.maximum(tile(i) * hb - 1, 0), 0)),
                  pl.BlockSpec((POOL_HALO, POOL_WIDTH),
                               lambda i: (jnp.minimum((tile(i) + 1) * hb, n_halo_blocks - 1), 0)),
                  pl.BlockSpec((tm, GLA_WIDTH), row),
                  pl.BlockSpec((tm, GLA_WIDTH), row),
                  pl.BlockSpec((tm, D_MODEL), row),
                  pl.BlockSpec((1, 3, D_MODEL), sel),
                  pl.BlockSpec((1, GLA_DV), const2),
                  pl.BlockSpec((1, D_MODEL), const2),
                  pl.BlockSpec((2, 2 * POOL_GROUP, 2 * POOL_GROUP), const3),
                  pl.BlockSpec((1, POOL_WIDTH), const2),
                  pl.BlockSpec((MLA_WIDTH, D_MODEL), const2),
                  pl.BlockSpec((POOL_WIDTH, D_MODEL), const2),
                  pl.BlockSpec((GLA_WIDTH, D_MODEL), const2),
                  pl.BlockSpec((D_MODEL, D_MODEL), const2)],
        out_specs=pl.BlockSpec((tm, D_MODEL), lambda i: (i, 0)),
        out_shape=jax.ShapeDtypeStruct((n_out_tiles * tm, D_MODEL), F32),
        scratch_shapes=[pltpu.VMEM((tm + 2 * POOL_HALO, POOL_WIDTH), F32)],
        compiler_params=pltpu.CompilerParams(dimension_semantics=("parallel",),
                                             vmem_limit_bytes=VMEM_LIMIT),
        name="merge",
    )(att, gates, pool_x, pool_x, pool_x, o_f, o_b, x_all, ss, gla_norm, post_g, pool_w2, pool_scale,
      w_bm, w_bp, w_bg, w_out)


def _rot_cols(w):
    half = MLA_ROPE // 2
    quarter = half // 2

    def rot(v):
        return jnp.concatenate([-v[..., quarter:], v[..., :quarter]], axis=-1)

    return jnp.concatenate([rot(w[..., :half]), rot(w[..., half:])], axis=-1)


def _prep_w_in(w):
    offs = [0]
    for s in IN_SIZES:
        offs.append(offs[-1] + s)
    seg = {n: w[:, offs[i]:offs[i + 1]] for i, n in enumerate(
        ('mla_q', 'mla_kv', 'mla_kr', 'mla_gate', 'pool_x', 'pool_gate', 'gla_q', 'gla_k', 'gla_v',
         'gla_af', 'gla_ab', 'gla_gate', 'merge'))}
    d = w.shape[0]
    z = lambda n: jnp.zeros((d, n), w.dtype)
    kr_pad = jnp.concatenate([z(MLA_NOPE), seg['mla_kr'], z(HEAD_PAD - MLA_NOPE - MLA_ROPE)], axis=1)
    kr_rot = jnp.concatenate([z(MLA_NOPE), _rot_cols(seg['mla_kr']), z(HEAD_PAD - MLA_NOPE - MLA_ROPE)], axis=1)
    gate_in = jnp.concatenate([seg['gla_af'], seg['gla_ab'], z(128 - 2 * GLA_GATE_RANK)], axis=1)
    out = jnp.concatenate([seg['mla_q'], seg['mla_kv'], kr_pad, kr_rot,
                           seg['gla_q'], seg['gla_k'], seg['gla_v'], gate_in,
                           seg['pool_x'],
                           seg['mla_gate'], seg['pool_gate'], seg['gla_gate'], seg['merge']], axis=1)
    return out.astype(BF16)


def _prep_w_uq(w):
    r = w.shape[0]
    w3 = w.reshape(r, MLA_HEADS, MLA_NOPE + MLA_ROPE)
    nope, rope = w3[..., :MLA_NOPE], w3[..., MLA_NOPE:]
    pad = jnp.zeros((r, MLA_HEADS, HEAD_PAD - MLA_NOPE - MLA_ROPE), w.dtype)
    plain = jnp.concatenate([nope, rope, pad], axis=-1).reshape(r, MLA_HEADS * HEAD_PAD)
    rotated = jnp.concatenate([jnp.zeros_like(nope), _rot_cols(rope), pad], axis=-1).reshape(r, MLA_HEADS * HEAD_PAD)
    return jnp.concatenate([plain, rotated], axis=1).astype(BF16)


def _prep_w_ukv(w):
    r = w.shape[0]
    w3 = w.reshape(r, MLA_HEADS, MLA_NOPE + MLA_V)
    k_nope, v = w3[..., :MLA_NOPE], w3[..., MLA_NOPE:]
    zk = jnp.zeros((r, MLA_HEADS, HEAD_PAD - MLA_NOPE), w.dtype)
    k_pad = jnp.concatenate([k_nope, zk], axis=-1).reshape(r, MLA_HEADS * HEAD_PAD)
    zv = jnp.zeros_like(v)
    v_even = jnp.concatenate([v, zv], axis=-1)
    v_odd = jnp.concatenate([zv, v], axis=-1)
    odd = (jnp.arange(MLA_HEADS) % 2 == 1)[None, :, None]
    v_pad = jnp.where(odd, v_odd, v_even).reshape(r, MLA_HEADS * HEAD_PAD)
    return jnp.concatenate([k_pad, v_pad], axis=1).astype(BF16)


def _prep_gate_w(w2, slot):
    full = jnp.zeros((128, GLA_KW), w2.dtype)
    return full.at[slot * GLA_GATE_RANK:(slot + 1) * GLA_GATE_RANK].set(w2).astype(BF16)


def _prep_pool_w(pw):
    z = jnp.zeros((POOL_GROUP, POOL_GROUP), pw.dtype)
    blocks = [jnp.concatenate([jnp.concatenate([pw[2 * j], z], axis=1),
                               jnp.concatenate([z, pw[2 * j + 1]], axis=1)], axis=0) for j in range(2)]
    return jnp.stack(blocks).astype(BF16)


def _rope_tables(lat_len, ctx_len):
    half = MLA_ROPE // 2
    rows = lat_len // GRID_W
    row = jnp.repeat(jnp.arange(rows), GRID_W).astype(F32)
    col = jnp.tile(jnp.arange(GRID_W), rows).astype(F32)
    inv = ROPE_BASE ** (-jnp.arange(0, half, 2, dtype=F32) / half)
    ang_r = row[:, None] * inv
    ang_c = col[:, None] * inv
    ang = jnp.concatenate([ang_r, ang_r, ang_c, ang_c], axis=-1)
    cos = jnp.concatenate([jnp.ones((ctx_len, MLA_ROPE), F32), jnp.cos(ang)], axis=0)
    sin = jnp.concatenate([jnp.zeros((ctx_len, MLA_ROPE), F32), jnp.sin(ang)], axis=0)
    n = lat_len + ctx_len
    scale = (MLA_NOPE + MLA_ROPE) ** -0.5 * 1.4426950408889634
    tail = jnp.zeros((n, HEAD_PAD - MLA_NOPE - MLA_ROPE), F32)
    cq = jnp.concatenate([jnp.ones((n, MLA_NOPE), F32), cos, tail], axis=1) * scale
    sq = jnp.concatenate([jnp.zeros((n, MLA_NOPE), F32), sin, tail], axis=1) * scale
    ck = jnp.concatenate([jnp.zeros((n, MLA_NOPE), F32), cos, tail], axis=1)
    sk = jnp.concatenate([jnp.zeros((n, MLA_NOPE), F32), sin, tail], axis=1)
    return cq, sq, ck, sk


def kernel(x, c, ctx, c_ctx, mod_w, mod_b, pre_norm, post_norm, w_in, mla_q_norm, mla_w_uq, mla_kv_norm, mla_w_ukv, pool_w, pool_scale, gla_af_w2, gla_af_b, gla_ab_w2, gla_ab_b, gla_norm, w_branch_mla, w_branch_pool, w_branch_gla, w_out):
    batch, lat_len, d = x.shape
    ctx_len = ctx.shape[1]
    depth = mod_w.shape[0]
    l_all = lat_len + ctx_len
    assert d == D_MODEL and ctx_len % TOKEN_TILE == 0 and lat_len % TOKEN_TILE == 0
    assert lat_len % GRID_W == 0 and ctx_len % GLA_CHUNK == 0
    ctx_tiles = ctx_len // TOKEN_TILE

    mp = -(-(batch + 1) // 8) * 8
    c_all = jnp.concatenate([c, c_ctx[None], jnp.zeros((mp - batch - 1, d), c.dtype)], axis=0)
    mod_all = _modulation(c_all, mod_w.astype(BF16), mod_b)

    tables = _rope_tables(lat_len, ctx_len)
    x_all = jnp.concatenate([ctx, x], axis=1).reshape(batch * l_all, d)

    for l in range(depth):
        last = l == depth - 1
        mod_lat = mod_all[l, :batch].reshape(batch, 1, 3, d)
        mod_ctx = jnp.broadcast_to(mod_all[l, batch].reshape(1, 1, 3, d), (batch, 1, 3, d))
        ss = jnp.concatenate([mod_ctx, mod_lat], axis=1).reshape(batch * 2, 3, d)

        q, k, v, gla_in, pool_x, gates = _inproj(
            x_all, ss, pre_norm[l][None], _prep_w_in(w_in[l]), mla_q_norm[l][None], _prep_w_uq(mla_w_uq[l]),
            mla_kv_norm[l][None], _prep_w_ukv(mla_w_ukv[l]), tables, batch, l_all, ctx_tiles)
        att = _attention(q, k, v, ctx_len)
        o_f, o_b = _gla(gla_in, _prep_gate_w(gla_af_w2[l], 0), gla_af_b[l][None],
                        _prep_gate_w(gla_ab_w2[l], 1), gla_ab_b[l][None], batch, l_all, ctx_len)
        x_all = _merge(att, gates, pool_x, o_f, o_b, x_all, ss, gla_norm[l][None], post_norm[l][None],
                       _prep_pool_w(pool_w[l]), pool_scale[l][None],
                       w_branch_mla[l].astype(BF16), w_branch_pool[l].astype(BF16),
                       w_branch_gla[l].astype(BF16), w_out[l].astype(BF16),
                       batch, l_all, ctx_len, latent_only=last)
    return x_all.reshape(batch, lat_len, d)
```

```python
import functools

import jax
import jax.numpy as jnp
from jax import lax
from jax.experimental import pallas as pl
from jax.experimental.pallas import tpu as pltpu

F32 = jnp.float32
BF16 = jnp.bfloat16

D_MODEL = 1024
NORM_EPS = 1e-6
GRID_W = 64
ROPE_BASE = 10000.0

MLA_HEADS = 8
MLA_Q_RANK = 256
MLA_KV_RANK = 128
MLA_NOPE = 64
MLA_ROPE = 32
MLA_V = 64
MLA_WIDTH = MLA_HEADS * MLA_V
HEAD_PAD = 128

POOL_WINDOWS = (2, 4, 8, 16)
POOL_WIDTH = 512
POOL_GROUP = 128
POOL_HALO = 16

GLA_HEADS = 4
GLA_DK = 64
GLA_DV = 128
GLA_KW = GLA_HEADS * GLA_DK
GLA_WIDTH = GLA_HEADS * GLA_DV
GLA_GATE_RANK = 16
GLA_TAU = 16.0
GLA_CHUNK = 128
GLA_PAIRS = GLA_HEADS // 2

N_BRANCH = 3
IN_SIZES = (MLA_Q_RANK, MLA_KV_RANK, MLA_ROPE, MLA_WIDTH, POOL_WIDTH, POOL_WIDTH,
            GLA_KW, GLA_KW, GLA_WIDTH, GLA_GATE_RANK, GLA_GATE_RANK, GLA_WIDTH, N_BRANCH * D_MODEL)
IN_NAMES = ('mla_q', 'mla_kv', 'mla_kr', 'mla_gate', 'pool_x', 'pool_gate', 'gla_q', 'gla_k', 'gla_v',
            'gla_af', 'gla_ab', 'gla_gate', 'merge')

C_Q = 0
C_KV = C_Q + MLA_Q_RANK
C_KR = C_KV + MLA_KV_RANK
C_GLA = C_KR + 2 * HEAD_PAD
GLA_IN_W = 2 * GLA_KW + GLA_WIDTH + 128
C_POOL = C_GLA + GLA_IN_W
C_END = C_POOL + POOL_WIDTH
GATES_W = 3 * 512 + N_BRANCH * D_MODEL

LATENT_TILE = 512
ATTN_TILE = 256
GLA_BATCH_BLOCK = 4
VMEM_LIMIT = 56 * 1024 * 1024


def _sigmoid(x):
    return 0.5 * jnp.tanh(0.5 * x) + 0.5


def _silu(x):
    h = 0.5 * x
    return h * jnp.tanh(h) + h


def _rmsnorm(x, g):
    return x * lax.rsqrt(jnp.mean(x * x, axis=-1, keepdims=True) + NORM_EPS) * g


def _dot(a, b):
    return jnp.dot(a, b, preferred_element_type=F32)


def _dot_nt(a, b):
    return lax.dot_general(a, b, (((1,), (1,)), ((), ())), preferred_element_type=F32)


def _dot_tn(a, b):
    return lax.dot_general(a, b, (((0,), (0,)), ((), ())), preferred_element_type=F32)


def _modulated(x, ss, pre_g):
    return (_rmsnorm(x, pre_g) * (1.0 + ss[1:2]) + ss[0:1]).astype(BF16)


def _mod_kernel(c_ref, w_ref, b_ref, o_ref):
    c = c_ref[...]
    s = (c / (1.0 + jnp.exp(-c))).astype(BF16)
    o_ref[0] = _dot(s, w_ref[0]) + b_ref[0]


def _modulation(c_all, mod_w, mod_b):
    depth, d, n = mod_w.shape
    mp = c_all.shape[0]
    tn = 1024
    return pl.pallas_call(
        _mod_kernel,
        grid=(depth, n // tn),
        in_specs=[pl.BlockSpec((mp, d), lambda l, j: (0, 0)),
                  pl.BlockSpec((1, d, tn), lambda l, j: (l, 0, j)),
                  pl.BlockSpec((1, 1, tn), lambda l, j: (l, 0, j))],
        out_specs=pl.BlockSpec((1, mp, tn), lambda l, j: (l, 0, j)),
        out_shape=jax.ShapeDtypeStruct((depth, mp, n), F32),
        name="modulation",
    )(c_all, mod_w, mod_b.reshape(depth, 1, n))


def _inproj_kernel(x_ref, ss_ref, pre_ref, w_ref, qn_ref, wuq_ref, kvn_ref, wukv_ref,
                   cq_ref, sq_ref, ck_ref, sk_ref,
                   q_ref, k_ref, v_ref, gla_ref, pool_ref):
    xb = _modulated(x_ref[0], ss_ref[0], pre_ref[...])
    gla_ref[0] = _dot(xb, w_ref[:, C_GLA:C_POOL]).astype(gla_ref.dtype)
    pool_ref[0] = _dot(xb, w_ref[:, C_POOL:C_END]).astype(pool_ref.dtype)

    hw = MLA_HEADS * HEAD_PAD
    zq = _dot(xb, w_ref[:, C_Q:C_KV])
    qq = _dot(_rmsnorm(zq, qn_ref[...]).astype(BF16), wuq_ref[...])
    cq, sq = cq_ref[...], sq_ref[...]
    for h in range(MLA_HEADS):
        lo = h * HEAD_PAD
        q_ref[0, h] = (qq[:, lo:lo + HEAD_PAD] * cq + qq[:, hw + lo:hw + lo + HEAD_PAD] * sq).astype(q_ref.dtype)

    zkv = _dot(xb, w_ref[:, C_KV:C_KR])
    zkr = _dot(xb, w_ref[:, C_KR:C_GLA])
    k_rope = zkr[:, :HEAD_PAD] * ck_ref[...] + zkr[:, HEAD_PAD:] * sk_ref[...]
    kv = _dot(_rmsnorm(zkv, kvn_ref[...]).astype(BF16), wukv_ref[...])
    for h in range(MLA_HEADS):
        lo = h * HEAD_PAD
        k_ref[0, h] = (kv[:, lo:lo + HEAD_PAD] + k_rope).astype(k_ref.dtype)
        v_ref[0, h] = kv[:, hw + lo:hw + lo + HEAD_PAD].astype(v_ref.dtype)


def _inproj(x, ss, lw, tables, tm):
    batch, n, d = x.shape
    cq, sq, ck, sk = tables
    hw = MLA_HEADS * HEAD_PAD
    per_batch_ss = ss.shape[0] == batch
    const = lambda b, j: (0, 0)
    tab = pl.BlockSpec((tm, HEAD_PAD), lambda b, j: (j, 0))
    head_spec = pl.BlockSpec((1, MLA_HEADS, tm, HEAD_PAD), lambda b, j: (b, 0, j, 0))
    head_shape = jax.ShapeDtypeStruct((batch, MLA_HEADS, n, HEAD_PAD), BF16)
    row = lambda b, j: (b, j, 0)
    return pl.pallas_call(
        _inproj_kernel,
        grid=(batch, n // tm),
        in_specs=[pl.BlockSpec((1, tm, d), row),
                  pl.BlockSpec((1, 3, d), (lambda b, j: (b, 0, 0)) if per_batch_ss else (lambda b, j: (0, 0, 0))),
                  pl.BlockSpec((1, d), const),
                  pl.BlockSpec((d, C_END), const),
                  pl.BlockSpec((1, MLA_Q_RANK), const),
                  pl.BlockSpec((MLA_Q_RANK, 2 * hw), const),
                  pl.BlockSpec((1, MLA_KV_RANK), const),
                  pl.BlockSpec((MLA_KV_RANK, 2 * hw), const),
                  tab, tab, tab, tab],
        out_specs=[head_spec, head_spec, head_spec,
                   pl.BlockSpec((1, tm, GLA_IN_W), row),
                   pl.BlockSpec((1, tm, POOL_WIDTH), row)],
        out_shape=[head_shape, head_shape, head_shape,
                   jax.ShapeDtypeStruct((batch, n, GLA_IN_W), BF16),
                   jax.ShapeDtypeStruct((batch, n, POOL_WIDTH), BF16)],
        compiler_params=pltpu.CompilerParams(dimension_semantics=("parallel", "parallel"),
                                             vmem_limit_bytes=VMEM_LIMIT),
        name="inproj",
    )(x, ss, lw['pre_g'], lw['w_in'], lw['q_norm'], lw['w_uq'], lw['kv_norm'], lw['w_ukv'], cq, sq, ck, sk)


def _attn_kernel(q_ref, *refs, n_kv):
    kv_refs, o_ref = refs[:2 * n_kv], refs[2 * n_kv]

    def scores(h):
        q = q_ref[0, h]
        return [_dot_nt(q, kv_refs[2 * i][0, h]) for i in range(n_kv)]

    def weighted_values(h, s):
        m = functools.reduce(jnp.maximum, [jnp.max(t, axis=-1, keepdims=True) for t in s])
        e = [jnp.exp2(t - m) for t in s]
        l = functools.reduce(jnp.add, [jnp.sum(t, axis=-1, keepdims=True) for t in e])
        pv = functools.reduce(jnp.add, [_dot(e[i].astype(BF16), kv_refs[2 * i + 1][0, h]) for i in range(n_kv)])
        return pv * (1.0 / l)

    s_next = scores(0)
    acc = None
    for h in range(MLA_HEADS):
        s = s_next
        if h + 1 < MLA_HEADS:
            s_next = scores(h + 1)
        t = weighted_values(h, s)
        if h % 2 == 0:
            acc = t
        else:
            o_ref[0, :, (h // 2) * HEAD_PAD:(h // 2 + 1) * HEAD_PAD] = (acc + t).astype(o_ref.dtype)


def _attention(q, kvs):
    batch, heads, n, hp = q.shape
    tq = ATTN_TILE
    in_specs = [pl.BlockSpec((1, heads, tq, hp), lambda b, j: (b, 0, j, 0))]
    args = [q]
    for k, v in kvs:
        spec = pl.BlockSpec((1, heads, k.shape[2], hp), lambda b, j: (b, 0, 0, 0))
        in_specs += [spec, spec]
        args += [k, v]
    return pl.pallas_call(
        functools.partial(_attn_kernel, n_kv=len(kvs)),
        grid=(batch, n // tq),
        in_specs=in_specs,
        out_specs=pl.BlockSpec((1, tq, MLA_WIDTH), lambda b, j: (b, j, 0)),
        out_shape=jax.ShapeDtypeStruct((batch, n, MLA_WIDTH), BF16),
        compiler_params=pltpu.CompilerParams(dimension_semantics=("parallel", "arbitrary"),
                                             vmem_limit_bytes=VMEM_LIMIT),
        name="attention",
    )(*args)


def _gla_kernel(*refs, bb, has_init, with_out):
    g_refs = refs[0:2]
    wg_refs = (refs[2], refs[4])
    bg_refs = (refs[3], refs[5])
    pos = 6
    init_ref = None
    if has_init:
        init_ref = refs[pos]
        pos += 1
    o_refs = None
    if with_out:
        o_refs = (refs[pos], refs[pos + 1])
        pos += 2
    st_ref = refs[pos]

    @pl.when(pl.program_id(1) == 0)
    def _():
        if has_init:
            st_ref[...] = init_ref[...]
        else:
            st_ref[...] = jnp.zeros_like(st_ref)

    c = g_refs[0].shape[1]
    row = lax.broadcasted_iota(jnp.int32, (c, c), 0)
    col = lax.broadcasted_iota(jnp.int32, (c, c), 1)
    tri = (col <= row, col >= row)
    ones_tri = [jnp.where(t, 1.0, 0.0).astype(BF16) for t in tri]
    first = lax.broadcasted_iota(jnp.int32, (c, 2 * GLA_DK), 1) < GLA_DK
    first_st = lax.broadcasted_iota(jnp.int32, (GLA_DV, 2 * GLA_DK), 1) < GLA_DK
    inst = [(i, d) for i in range(bb) for d in range(2)]

    log_a = {}
    for d in range(2):
        gate_in = jnp.concatenate([g_refs[d][i, :, 2 * GLA_KW + GLA_WIDTH:] for i in range(bb)], axis=0)
        pre = _dot(gate_in, wg_refs[d][...]) + bg_refs[d][...]
        la = (jnp.minimum(pre, 0.0) - jnp.log1p(jnp.exp(-jnp.abs(pre)))) * (1.0 / GLA_TAU)
        for i in range(bb):
            log_a[i, d] = la[i * c:(i + 1) * c]

    cum = {}
    for d in range(2):
        la = jnp.concatenate([log_a[i, d] for i in range(bb)], axis=1)
        hi = la.astype(BF16)
        r1 = la - hi.astype(F32)
        mid = r1.astype(BF16)
        lo = (r1 - mid.astype(F32)).astype(BF16)
        b = _dot(ones_tri[d], hi) + _dot(ones_tri[d], mid) + _dot(ones_tri[d], lo)
        for i in range(bb):
            cum[i, d] = b[:, i * GLA_KW:(i + 1) * GLA_KW]

    k_t, dec, q_inter, q_intra = {}, {}, {}, {}
    for i, d in inst:
        b = cum[i, d]
        b_tot = b[0:1] if d == 1 else b[c - 1:c]
        k_t[i, d] = (g_refs[d][i, :, GLA_KW:2 * GLA_KW].astype(F32) * jnp.exp(b_tot - b)).astype(BF16)
        dec[i, d] = jnp.exp(b_tot)
        if with_out:
            q = g_refs[d][i, :, 0:GLA_KW].astype(F32) * (GLA_DK ** -0.5)
            q_inter[i, d] = q * jnp.exp(b)
            q_intra[i, d] = q * jnp.exp(b - b_tot)

    att = {}
    if with_out:
        for i, d in inst:
            for h in range(GLA_HEADS):
                sl = slice((h // 2) * 2 * GLA_DK, (h // 2 + 1) * 2 * GLA_DK)
                mask = first if h % 2 == 0 else jnp.logical_not(first)
                qi = jnp.where(mask, q_intra[i, d][:, sl], 0.0).astype(BF16)
                att[i, d, h] = jnp.where(tri[d], _dot_nt(qi, k_t[i, d][:, sl]), 0.0).astype(BF16)

    for i, d in inst:
        for p in range(GLA_PAIRS):
            sl = slice(p * 2 * GLA_DK, (p + 1) * 2 * GLA_DK)
            st = st_ref[i, d, p]
            kt = k_t[i, d][:, sl]
            upd = []
            for hh in range(2):
                h = 2 * p + hh
                vh = g_refs[d][i, :, 2 * GLA_KW + h * GLA_DV:2 * GLA_KW + (h + 1) * GLA_DV]
                if with_out:
                    mask = first if hh == 0 else jnp.logical_not(first)
                    qe = jnp.where(mask, q_inter[i, d][:, sl], 0.0).astype(BF16)
                    o = _dot_nt(qe, st.astype(BF16)) + _dot(att[i, d, h], vh)
                    o_refs[d][i, :, h * GLA_DV:(h + 1) * GLA_DV] = o.astype(o_refs[d].dtype)
                upd.append(_dot_tn(vh, kt))
            st_ref[i, d, p] = st * dec[i, d][:, sl] + jnp.where(first_st, upd[0], upd[1])


def _gla(gla_in, lw, init, with_out):
    batch, n, _ = gla_in.shape
    c = GLA_CHUNK
    nch = n // c
    bb = GLA_BATCH_BLOCK
    const = lambda b, s: (0, 0)
    wspec = pl.BlockSpec((128, GLA_KW), const)
    bspec = pl.BlockSpec((1, GLA_KW), const)
    fwd = lambda b, s: (b, s, 0)
    bwd = lambda b, s: (b, nch - 1 - s, 0)
    st_shape = (batch, 2, GLA_PAIRS, GLA_DV, 2 * GLA_DK)
    st_spec = pl.BlockSpec((bb,) + st_shape[1:], lambda b, s: (b, 0, 0, 0, 0))
    in_specs = [pl.BlockSpec((bb, c, GLA_IN_W), fwd), pl.BlockSpec((bb, c, GLA_IN_W), bwd),
                wspec, bspec, wspec, bspec]
    args = [gla_in, gla_in, lw['w_af'], lw['b_af'], lw['w_ab'], lw['b_ab']]
    out_specs, out_shape, scratch = [], [], []
    if init is not None:
        in_specs.append(st_spec)
        args.append(init)
    if with_out:
        o_shape = jax.ShapeDtypeStruct((batch, n, GLA_WIDTH), F32)
        out_specs += [pl.BlockSpec((bb, c, GLA_WIDTH), fwd), pl.BlockSpec((bb, c, GLA_WIDTH), bwd)]
        out_shape += [o_shape, o_shape]
    if init is None:
        out_specs.append(st_spec)
        out_shape.append(jax.ShapeDtypeStruct(st_shape, F32))
    else:
        scratch.append(pltpu.VMEM((bb,) + st_shape[1:], F32))
    return pl.pallas_call(
        functools.partial(_gla_kernel, bb=bb, has_init=init is not None, with_out=with_out),
        grid=(batch // bb, nch),
        in_specs=in_specs,
        out_specs=out_specs,
        out_shape=out_shape,
        scratch_shapes=scratch,
        compiler_params=pltpu.CompilerParams(dimension_semantics=("parallel", "arbitrary"),
                                             vmem_limit_bytes=VMEM_LIMIT),
        name="gla_scan",
    )(*args)


def _merge_kernel(x_ref, ss_ref, pre_ref, att_ref, pc_ref, pp_ref, pn_ref, of_ref, ob_ref,
                  wg_ref, gn_ref, post_ref, pw_ref, ps_ref, wbm_ref, wbp_ref, wbg_ref, wout_ref,
                  o_ref, ubuf):
    tm = x_ref.shape[1]
    j = pl.program_id(1)
    seq_len = tm * pl.num_programs(1)
    x = x_ref[0]
    ss = ss_ref[0]
    xb = _modulated(x, ss, pre_ref[...])

    def gate_cols(lo, n):
        return _dot(xb, wg_ref[:, lo:lo + n])

    hal = POOL_HALO
    ubuf[0:hal] = jnp.where(j > 0, pp_ref[0].astype(F32), 0.0)
    ubuf[hal:hal + tm] = pc_ref[0].astype(F32)
    ubuf[hal + tm:hal + tm + hal] = jnp.where(j < pl.num_programs(1) - 1, pn_ref[0].astype(F32), 0.0)
    tpos = j * tm + lax.broadcasted_iota(jnp.int32, (tm, POOL_GROUP), 0)
    pooled, pool_gate, pool_merge_gate = [], [], []
    for g, w in enumerate(POOL_WINDOWS):
        if g < 2:
            pool_gate.append(_silu(gate_cols(512 + g * 256, 256)))
        pool_merge_gate.append(_sigmoid(gate_cols(1536 + D_MODEL + g * 256, 256)))
        lanes = slice(g * POOL_GROUP, (g + 1) * POOL_GROUP)
        acc = None
        for d in range(-(w // 2), w // 2):
            piece = ubuf[hal + d:hal + d + tm, lanes]
            acc = piece if acc is None else acc + piece
        count = (jnp.minimum(tpos + w // 2, seq_len) - jnp.maximum(tpos - w // 2, 0)).astype(F32)
        pooled.append((acc / count - ubuf[hal:hal + tm, lanes]).astype(BF16))
    pool_gate = jnp.concatenate(pool_gate, axis=-1)
    pool_merge_gate = jnp.concatenate(pool_merge_gate, axis=-1)
    mixed = jnp.concatenate(
        [_dot(jnp.concatenate(pooled[2 * i:2 * i + 2], axis=-1), pw_ref[i]) for i in range(2)], axis=-1)
    y_pool = mixed * ps_ref[...] * pool_gate
    merged = pool_merge_gate * _dot(y_pool.astype(BF16), wbp_ref[...])

    y_mla = att_ref[0].astype(F32) * _silu(gate_cols(0, 512))
    merged = merged + _sigmoid(gate_cols(1536, D_MODEL)) * _dot(y_mla.astype(BF16), wbm_ref[...])

    o = of_ref[0] + ob_ref[0]
    gn = gn_ref[...]
    o = jnp.concatenate([_rmsnorm(o[:, h * GLA_DV:(h + 1) * GLA_DV], gn) for h in range(GLA_HEADS)], axis=-1)
    y_gla = o * _silu(gate_cols(1024, 512))
    merged = merged + _sigmoid(gate_cols(1536 + 2 * D_MODEL, D_MODEL)) * _dot(y_gla.astype(BF16), wbg_ref[...])
    out = _dot(merged.astype(BF16), wout_ref[...])
    o_ref[0] = x + ss[2:3] * _rmsnorm(out, post_ref[...])


def _merge(x, ss, att, pool_x, o_f, o_b, lw, tm):
    batch, n, d = x.shape
    hb = tm // POOL_HALO
    n_halo = n // POOL_HALO
    per_batch_ss = ss.shape[0] == batch
    row = lambda b, j: (b, j, 0)
    const2 = lambda b, j: (0, 0)
    const3 = lambda b, j: (0, 0, 0)
    single = dict(pipeline_mode=pl.Buffered(1))
    return pl.pallas_call(
        _merge_kernel,
        grid=(batch, n // tm),
        in_specs=[pl.BlockSpec((1, tm, d), row),
                  pl.BlockSpec((1, 3, d), (lambda b, j: (b, 0, 0)) if per_batch_ss else (lambda b, j: (0, 0, 0))),
                  pl.BlockSpec((1, d), const2),
                  pl.BlockSpec((1, tm, MLA_WIDTH), row),
                  pl.BlockSpec((1, tm, POOL_WIDTH), row),
                  pl.BlockSpec((1, POOL_HALO, POOL_WIDTH), lambda b, j: (b, jnp.maximum(j * hb - 1, 0), 0)),
                  pl.BlockSpec((1, POOL_HALO, POOL_WIDTH),
                               lambda b, j: (b, jnp.minimum((j + 1) * hb, n_halo - 1), 0)),
                  pl.BlockSpec((1, tm, GLA_WIDTH), row),
                  pl.BlockSpec((1, tm, GLA_WIDTH), row),
                  pl.BlockSpec((d, GATES_W), const2, **single),
                  pl.BlockSpec((1, GLA_DV), const2),
                  pl.BlockSpec((1, d), const2),
                  pl.BlockSpec((2, 2 * POOL_GROUP, 2 * POOL_GROUP), const3),
                  pl.BlockSpec((1, POOL_WIDTH), const2),
                  pl.BlockSpec((MLA_WIDTH, d), const2, **single),
                  pl.BlockSpec((POOL_WIDTH, d), const2, **single),
                  pl.BlockSpec((GLA_WIDTH, d), const2, **single),
                  pl.BlockSpec((d, d), const2, **single)],
        out_specs=pl.BlockSpec((1, tm, d), row),
        out_shape=jax.ShapeDtypeStruct((batch, n, d), F32),
        scratch_shapes=[pltpu.VMEM((tm + 2 * POOL_HALO, POOL_WIDTH), F32)],
        compiler_params=pltpu.CompilerParams(dimension_semantics=("parallel", "parallel"),
                                             vmem_limit_bytes=VMEM_LIMIT),
        name="merge",
    )(x, ss, lw['pre_g'], att, pool_x, pool_x, pool_x, o_f, o_b, lw['w_gates'], lw['gla_norm'], lw['post_g'],
      lw['pool_w'], lw['pool_scale'], lw['w_bm'], lw['w_bp'], lw['w_bg'], lw['w_out'])


def _rot_cols(w):
    half = MLA_ROPE // 2
    quarter = half // 2

    def rot(v):
        return jnp.concatenate([-v[..., quarter:], v[..., :quarter]], axis=-1)

    return jnp.concatenate([rot(w[..., :half]), rot(w[..., half:])], axis=-1)


def _prep_w_in(w):
    offs = [0]
    for s in IN_SIZES:
        offs.append(offs[-1] + s)
    seg = {n: w[:, offs[i]:offs[i + 1]] for i, n in enumerate(IN_NAMES)}
    d = w.shape[0]
    z = lambda n: jnp.zeros((d, n), w.dtype)
    kr_pad = jnp.concatenate([z(MLA_NOPE), seg['mla_kr'], z(HEAD_PAD - MLA_NOPE - MLA_ROPE)], axis=1)
    kr_rot = jnp.concatenate([z(MLA_NOPE), _rot_cols(seg['mla_kr']), z(HEAD_PAD - MLA_NOPE - MLA_ROPE)], axis=1)
    gate_in = jnp.concatenate([seg['gla_af'], seg['gla_ab'], z(128 - 2 * GLA_GATE_RANK)], axis=1)
    w_heads = jnp.concatenate([seg['mla_q'], seg['mla_kv'], kr_pad, kr_rot,
                               seg['gla_q'], seg['gla_k'], seg['gla_v'], gate_in, seg['pool_x']], axis=1)
    w_gates = jnp.concatenate([seg['mla_gate'], seg['pool_gate'], seg['gla_gate'], seg['merge']], axis=1)
    return w_heads.astype(BF16), w_gates.astype(BF16)


def _prep_w_uq(w):
    r = w.shape[0]
    w3 = w.reshape(r, MLA_HEADS, MLA_NOPE + MLA_ROPE)
    nope, rope = w3[..., :MLA_NOPE], w3[..., MLA_NOPE:]
    pad = jnp.zeros((r, MLA_HEADS, HEAD_PAD - MLA_NOPE - MLA_ROPE), w.dtype)
    plain = jnp.concatenate([nope, rope, pad], axis=-1).reshape(r, MLA_HEADS * HEAD_PAD)
    rotated = jnp.concatenate([jnp.zeros_like(nope), _rot_cols(rope), pad], axis=-1).reshape(r, MLA_HEADS * HEAD_PAD)
    return jnp.concatenate([plain, rotated], axis=1).astype(BF16)


def _prep_w_ukv(w):
    r = w.shape[0]
    w3 = w.reshape(r, MLA_HEADS, MLA_NOPE + MLA_V)
    k_nope, v = w3[..., :MLA_NOPE], w3[..., MLA_NOPE:]
    zk = jnp.zeros((r, MLA_HEADS, HEAD_PAD - MLA_NOPE), w.dtype)
    k_pad = jnp.concatenate([k_nope, zk], axis=-1).reshape(r, MLA_HEADS * HEAD_PAD)
    zv = jnp.zeros_like(v)
    v_even = jnp.concatenate([v, zv], axis=-1)
    v_odd = jnp.concatenate([zv, v], axis=-1)
    odd = (jnp.arange(MLA_HEADS) % 2 == 1)[None, :, None]
    v_pad = jnp.where(odd, v_odd, v_even).reshape(r, MLA_HEADS * HEAD_PAD)
    return jnp.concatenate([k_pad, v_pad], axis=1).astype(BF16)


def _prep_gate_w(w2, slot):
    full = jnp.zeros((128, GLA_KW), w2.dtype)
    return full.at[slot * GLA_GATE_RANK:(slot + 1) * GLA_GATE_RANK].set(w2).astype(BF16)


def _prep_pool_w(pw):
    z = jnp.zeros((POOL_GROUP, POOL_GROUP), pw.dtype)
    blocks = [jnp.concatenate([jnp.concatenate([pw[2 * j], z], axis=1),
                               jnp.concatenate([z, pw[2 * j + 1]], axis=1)], axis=0) for j in range(2)]
    return jnp.stack(blocks).astype(BF16)


def _head_tables(cos, sin):
    n = cos.shape[0]
    scale = (MLA_NOPE + MLA_ROPE) ** -0.5 * 1.4426950408889634
    tail = jnp.zeros((n, HEAD_PAD - MLA_NOPE - MLA_ROPE), F32)
    cq = jnp.concatenate([jnp.ones((n, MLA_NOPE), F32), cos, tail], axis=1) * scale
    sq = jnp.concatenate([jnp.zeros((n, MLA_NOPE), F32), sin, tail], axis=1) * scale
    ck = jnp.concatenate([jnp.zeros((n, MLA_NOPE), F32), cos, tail], axis=1)
    sk = jnp.concatenate([jnp.zeros((n, MLA_NOPE), F32), sin, tail], axis=1)
    return cq, sq, ck, sk


def _rope_tables(lat_len, ctx_len):
    half = MLA_ROPE // 2
    rows = lat_len // GRID_W
    row = jnp.repeat(jnp.arange(rows), GRID_W).astype(F32)
    col = jnp.tile(jnp.arange(GRID_W), rows).astype(F32)
    inv = ROPE_BASE ** (-jnp.arange(0, half, 2, dtype=F32) / half)
    ang_r = row[:, None] * inv
    ang_c = col[:, None] * inv
    ang = jnp.concatenate([ang_r, ang_r, ang_c, ang_c], axis=-1)
    lat = _head_tables(jnp.cos(ang), jnp.sin(ang))
    ctx = _head_tables(jnp.ones((ctx_len, MLA_ROPE), F32), jnp.zeros((ctx_len, MLA_ROPE), F32))
    return lat, ctx


def _layer_weights(l, pre_norm, post_norm, w_in, mla_q_norm, mla_w_uq, mla_kv_norm, mla_w_ukv, pool_w, pool_scale,
                   gla_af_w2, gla_af_b, gla_ab_w2, gla_ab_b, gla_norm, w_branch_mla, w_branch_pool, w_branch_gla,
                   w_out):
    w_heads, w_gates = _prep_w_in(w_in[l])
    return dict(
        pre_g=pre_norm[l][None], post_g=post_norm[l][None], w_in=w_heads, w_gates=w_gates,
        q_norm=mla_q_norm[l][None], w_uq=_prep_w_uq(mla_w_uq[l]),
        kv_norm=mla_kv_norm[l][None], w_ukv=_prep_w_ukv(mla_w_ukv[l]),
        pool_w=_prep_pool_w(pool_w[l]), pool_scale=pool_scale[l][None],
        w_af=_prep_gate_w(gla_af_w2[l], 0), b_af=gla_af_b[l][None],
        w_ab=_prep_gate_w(gla_ab_w2[l], 1), b_ab=gla_ab_b[l][None],
        gla_norm=gla_norm[l][None],
        w_bm=w_branch_mla[l].astype(BF16), w_bp=w_branch_pool[l].astype(BF16),
        w_bg=w_branch_gla[l].astype(BF16), w_out=w_out[l].astype(BF16))


def kernel(x, c, ctx, c_ctx, mod_w, mod_b, pre_norm, post_norm, w_in, mla_q_norm, mla_w_uq, mla_kv_norm, mla_w_ukv, pool_w, pool_scale, gla_af_w2, gla_af_b, gla_ab_w2, gla_ab_b, gla_norm, w_branch_mla, w_branch_pool, w_branch_gla, w_out):
    batch, lat_len, d = x.shape
    ctx_len = ctx.shape[1]
    depth = mod_w.shape[0]
    ctx_tile = min(ctx_len, LATENT_TILE)
    assert d == D_MODEL and lat_len % LATENT_TILE == 0 and ctx_len % ctx_tile == 0
    assert lat_len % GRID_W == 0 and ctx_len % ATTN_TILE == 0 and ctx_len % GLA_CHUNK == 0
    assert batch % GLA_BATCH_BLOCK == 0

    mp = -(-(batch + 1) // 8) * 8
    c_all = jnp.concatenate([c, c_ctx[None], jnp.zeros((mp - batch - 1, d), c.dtype)], axis=0)
    mod_all = _modulation(c_all, mod_w.astype(BF16), mod_b)
    tab_lat, tab_ctx = _rope_tables(lat_len, ctx_len)

    xc = ctx
    for l in range(depth):
        last = l == depth - 1
        lw = _layer_weights(l, pre_norm, post_norm, w_in, mla_q_norm, mla_w_uq, mla_kv_norm, mla_w_ukv, pool_w,
                            pool_scale, gla_af_w2, gla_af_b, gla_ab_w2, gla_ab_b, gla_norm, w_branch_mla,
                            w_branch_pool, w_branch_gla, w_out)
        ss = mod_all[l, :batch].reshape(batch, 3, d)
        ss_c = mod_all[l, batch].reshape(1, 3, d)

        qc, kc, vc, gla_c, pool_c = _inproj(xc, ss_c, lw, tab_ctx, ctx_tile)
        q, k, v, gla_x, pool_x = _inproj(x, ss, lw, tab_lat, LATENT_TILE)

        att = _attention(q, [(kc, vc), (k, v)])
        ctx_scan = _gla(gla_c, lw, None, with_out=not last)
        o_f, o_b = _gla(gla_x, lw, ctx_scan[-1], with_out=True)
        if not last:
            att_c = _attention(qc, [(kc, vc)])
            xc = _merge(xc, ss_c, att_c, pool_c, ctx_scan[0], ctx_scan[1], lw, ctx_tile)
        x = _merge(x, ss, att, pool_x, o_f, o_b, lw, LATENT_TILE)
    return x
```

```python
import functools

import jax
import jax.numpy as jnp
from jax import lax
from jax.experimental import pallas as pl
from jax.experimental.pallas import tpu as pltpu

F32 = jnp.float32
BF16 = jnp.bfloat16

D_MODEL = 1024
NORM_EPS = 1e-6
GRID_W = 64
ROPE_BASE = 10000.0

MLA_HEADS = 8
MLA_Q_RANK = 256
MLA_KV_RANK = 128
MLA_NOPE = 64
MLA_ROPE = 32
MLA_V = 64
MLA_WIDTH = MLA_HEADS * MLA_V
HEAD_PAD = 128

POOL_WINDOWS = (2, 4, 8, 16)
POOL_WIDTH = 512
POOL_GROUP = 128
POOL_HALO = 16

GLA_HEADS = 4
GLA_DK = 64
GLA_DV = 128
GLA_KW = GLA_HEADS * GLA_DK
GLA_WIDTH = GLA_HEADS * GLA_DV
GLA_GATE_RANK = 16
GLA_TAU = 16.0
GLA_CHUNK = 128
GLA_PAIRS = GLA_HEADS // 2

N_BRANCH = 3
IN_SIZES = (MLA_Q_RANK, MLA_KV_RANK, MLA_ROPE, MLA_WIDTH, POOL_WIDTH, POOL_WIDTH,
            GLA_KW, GLA_KW, GLA_WIDTH, GLA_GATE_RANK, GLA_GATE_RANK, GLA_WIDTH, N_BRANCH * D_MODEL)
IN_NAMES = ('mla_q', 'mla_kv', 'mla_kr', 'mla_gate', 'pool_x', 'pool_gate', 'gla_q', 'gla_k', 'gla_v',
            'gla_af', 'gla_ab', 'gla_gate', 'merge')

C_Q = 0
C_KV = C_Q + MLA_Q_RANK
C_KR = C_KV + MLA_KV_RANK
C_GLA = C_KR + 2 * HEAD_PAD
GLA_IN_W = 2 * GLA_KW + GLA_WIDTH + 128
C_POOL = C_GLA + GLA_IN_W
C_END = C_POOL + POOL_WIDTH
GATES_W = 3 * 512 + N_BRANCH * D_MODEL

LATENT_TILE = 512
ATTN_TILE = 512
SCORE_LOOKAHEAD = 1
GLA_BATCH_BLOCK = 8
VMEM_LIMIT = 56 * 1024 * 1024


def _sigmoid(x):
    return 0.5 * jnp.tanh(0.5 * x) + 0.5


def _silu(x):
    h = 0.5 * x
    return h * jnp.tanh(h) + h


def _rmsnorm(x, g):
    return x * lax.rsqrt(jnp.mean(x * x, axis=-1, keepdims=True) + NORM_EPS) * g


def _dot(a, b):
    return jnp.dot(a, b, preferred_element_type=F32)


def _dot_nt(a, b):
    return lax.dot_general(a, b, (((1,), (1,)), ((), ())), preferred_element_type=F32)


def _dot_tn(a, b):
    return lax.dot_general(a, b, (((0,), (0,)), ((), ())), preferred_element_type=F32)


def _modulated(x, ss, pre_g):
    return (_rmsnorm(x, pre_g) * (1.0 + ss[1:2]) + ss[0:1]).astype(BF16)


def _mod_kernel(c_ref, w_ref, b_ref, o_ref):
    c = c_ref[...]
    s = (c / (1.0 + jnp.exp(-c))).astype(BF16)
    o_ref[0] = _dot(s, w_ref[0]) + b_ref[0]


def _modulation(c_all, mod_w, mod_b):
    depth, d, n = mod_w.shape
    mp = c_all.shape[0]
    tn = 1024
    return pl.pallas_call(
        _mod_kernel,
        grid=(depth, n // tn),
        in_specs=[pl.BlockSpec((mp, d), lambda l, j: (0, 0)),
                  pl.BlockSpec((1, d, tn), lambda l, j: (l, 0, j)),
                  pl.BlockSpec((1, 1, tn), lambda l, j: (l, 0, j))],
        out_specs=pl.BlockSpec((1, mp, tn), lambda l, j: (l, 0, j)),
        out_shape=jax.ShapeDtypeStruct((depth, mp, n), F32),
        name="modulation",
    )(c_all, mod_w, mod_b.reshape(depth, 1, n))


def _inproj_kernel(x_ref, ss_ref, pre_ref, w_ref, qn_ref, wuq_ref, kvn_ref, wukv_ref,
                   cq_ref, sq_ref, ck_ref, sk_ref,
                   q_ref, k_ref, v_ref, gla_ref, pool_ref, xb_ref):
    xb = _modulated(x_ref[0], ss_ref[0], pre_ref[...])
    xb_ref[0] = xb
    gla_ref[0] =_dot(xb, w_ref[:, C_GLA:C_POOL]).astype(gla_ref.dtype)
    pool_ref[0] = _dot(xb, w_ref[:, C_POOL:C_END]).astype(pool_ref.dtype)

    hw = MLA_HEADS * HEAD_PAD
    zq = _dot(xb, w_ref[:, C_Q:C_KV])
    qq = _dot(_rmsnorm(zq, qn_ref[...]).astype(BF16), wuq_ref[...])
    cq, sq = cq_ref[...], sq_ref[...]
    for h in range(MLA_HEADS):
        lo = h * HEAD_PAD
        q_ref[0, h] = (qq[:, lo:lo + HEAD_PAD] * cq + qq[:, hw + lo:hw + lo + HEAD_PAD] * sq).astype(q_ref.dtype)

    zkv = _dot(xb, w_ref[:, C_KV:C_KR])
    zkr = _dot(xb, w_ref[:, C_KR:C_GLA])
    k_rope = zkr[:, :HEAD_PAD] * ck_ref[...] + zkr[:, HEAD_PAD:] * sk_ref[...]
    kv = _dot(_rmsnorm(zkv, kvn_ref[...]).astype(BF16), wukv_ref[...])
    for h in range(MLA_HEADS):
        lo = h * HEAD_PAD
        k_ref[0, h] = (kv[:, lo:lo + HEAD_PAD] + k_rope).astype(k_ref.dtype)
        v_ref[0, h] = kv[:, hw + lo:hw + lo + HEAD_PAD].astype(v_ref.dtype)


def _inproj(x, ss, lw, tables, tm):
    batch, n, d = x.shape
    cq, sq, ck, sk = tables
    hw = MLA_HEADS * HEAD_PAD
    per_batch_ss = ss.shape[0] == batch
    const = lambda b, j: (0, 0)
    tab = pl.BlockSpec((tm, HEAD_PAD), lambda b, j: (j, 0))
    head_spec = pl.BlockSpec((1, MLA_HEADS, tm, HEAD_PAD), lambda b, j: (b, 0, j, 0))
    head_shape = jax.ShapeDtypeStruct((batch, MLA_HEADS, n, HEAD_PAD), BF16)
    row = lambda b, j: (b, j, 0)
    return pl.pallas_call(
        _inproj_kernel,
        grid=(batch, n // tm),
        in_specs=[pl.BlockSpec((1, tm, d), row),
                  pl.BlockSpec((1, 3, d), (lambda b, j: (b, 0, 0)) if per_batch_ss else (lambda b, j: (0, 0, 0))),
                  pl.BlockSpec((1, d), const),
                  pl.BlockSpec((d, C_END), const),
                  pl.BlockSpec((1, MLA_Q_RANK), const),
                  pl.BlockSpec((MLA_Q_RANK, 2 * hw), const),
                  pl.BlockSpec((1, MLA_KV_RANK), const),
                  pl.BlockSpec((MLA_KV_RANK, 2 * hw), const),
                  tab, tab, tab, tab],
        out_specs=[head_spec, head_spec, head_spec,
                   pl.BlockSpec((1, tm, GLA_IN_W), row),
                   pl.BlockSpec((1, tm, POOL_WIDTH), row),
                   pl.BlockSpec((1, tm, d), row)],
        out_shape=[head_shape, head_shape, head_shape,
                   jax.ShapeDtypeStruct((batch, n, GLA_IN_W), BF16),
                   jax.ShapeDtypeStruct((batch, n, POOL_WIDTH), BF16),
                   jax.ShapeDtypeStruct((batch, n, d), BF16)],
        compiler_params=pltpu.CompilerParams(dimension_semantics=("parallel", "parallel"),
                                             vmem_limit_bytes=VMEM_LIMIT),
        name="inproj",
    )(x, ss, lw['pre_g'], lw['w_in'], lw['q_norm'], lw['w_uq'], lw['kv_norm'], lw['w_ukv'], cq, sq, ck, sk)


def _attn_kernel(q_ref, *refs, n_kv):
    kv_refs, o_ref = refs[:2 * n_kv], refs[2 * n_kv]

    def scores(h):
        q = q_ref[0, h]
        return [_dot_nt(q, kv_refs[2 * i][0, h]) for i in range(n_kv)]

    def weighted_values(h, s):
        m = functools.reduce(jnp.maximum, [jnp.max(t, axis=-1, keepdims=True) for t in s])
        e = [jnp.exp2(t - m) for t in s]
        l = functools.reduce(jnp.add, [jnp.sum(t, axis=-1, keepdims=True) for t in e])
        pv = functools.reduce(jnp.add, [_dot(e[i].astype(BF16), kv_refs[2 * i + 1][0, h]) for i in range(n_kv)])
        return pv * (1.0 / l)

    pending = [scores(h) for h in range(SCORE_LOOKAHEAD)]
    acc = None
    for h in range(MLA_HEADS):
        if h + SCORE_LOOKAHEAD < MLA_HEADS:
            pending.append(scores(h + SCORE_LOOKAHEAD))
        t = weighted_values(h, pending.pop(0))
        if h % 2 == 0:
            acc = t
        else:
            o_ref[0, :, (h // 2) * HEAD_PAD:(h // 2 + 1) * HEAD_PAD] = (acc + t).astype(o_ref.dtype)


def _attention(q, kvs):
    batch, heads, n, hp = q.shape
    tq = min(ATTN_TILE, n)
    assert n % tq == 0
    in_specs = [pl.BlockSpec((1, heads, tq, hp), lambda b, j: (b, 0, j, 0))]
    args = [q]
    for k, v in kvs:
        spec = pl.BlockSpec((1, heads, k.shape[2], hp), lambda b, j: (b, 0, 0, 0))
        in_specs += [spec, spec]
        args += [k, v]
    return pl.pallas_call(
        functools.partial(_attn_kernel, n_kv=len(kvs)),
        grid=(batch, n // tq),
        in_specs=in_specs,
        out_specs=pl.BlockSpec((1, tq, MLA_WIDTH), lambda b, j: (b, j, 0)),
        out_shape=jax.ShapeDtypeStruct((batch, n, MLA_WIDTH), BF16),
        compiler_params=pltpu.CompilerParams(dimension_semantics=("parallel", "arbitrary"),
                                             vmem_limit_bytes=VMEM_LIMIT),
        name="attention",
    )(*args)


def _gla_kernel(*refs, bb, has_init, with_out):
    g_refs = refs[0:2]
    wg_refs = (refs[2], refs[4])
    bg_refs = (refs[3], refs[5])
    pos = 6
    init_ref = None
    if has_init:
        init_ref = refs[pos]
        pos += 1
    o_refs = None
    if with_out:
        o_refs = (refs[pos], refs[pos + 1])
        pos += 2
    st_ref = refs[pos]

    @pl.when(pl.program_id(1) == 0)
    def _():
        if has_init:
            st_ref[...] = init_ref[...]
        else:
            st_ref[...] = jnp.zeros_like(st_ref)

    c = g_refs[0].shape[1]
    row = lax.broadcasted_iota(jnp.int32, (c, c), 0)
    col = lax.broadcasted_iota(jnp.int32, (c, c), 1)
    tri = (col <= row, col >= row)
    ones_tri = [jnp.where(t, 1.0, 0.0).astype(BF16) for t in tri]
    first = lax.broadcasted_iota(jnp.int32, (c, 2 * GLA_DK), 1) < GLA_DK
    first_st = lax.broadcasted_iota(jnp.int32, (GLA_DV, 2 * GLA_DK), 1) < GLA_DK
    inst = [(i, d) for i in range(bb) for d in range(2)]

    log_a = {}
    for d in range(2):
        gate_in = jnp.concatenate([g_refs[d][i, :, 2 * GLA_KW + GLA_WIDTH:] for i in range(bb)], axis=0)
        pre = _dot(gate_in, wg_refs[d][...]) + bg_refs[d][...]
        la = (jnp.minimum(pre, 0.0) - jnp.log1p(jnp.exp(-jnp.abs(pre)))) * (1.0 / GLA_TAU)
        for i in range(bb):
            log_a[i, d] = la[i * c:(i + 1) * c]

    cum = {}
    for d in range(2):
        la = jnp.concatenate([log_a[i, d] for i in range(bb)], axis=1)
        hi = la.astype(BF16)
        r1 = la - hi.astype(F32)
        mid = r1.astype(BF16)
        lo = (r1 - mid.astype(F32)).astype(BF16)
        b = _dot(ones_tri[d], hi) + _dot(ones_tri[d], mid) + _dot(ones_tri[d], lo)
        for i in range(bb):
            cum[i, d] = b[:, i * GLA_KW:(i + 1) * GLA_KW]

    k_t, dec, q_inter, q_intra = {}, {}, {}, {}
    for i, d in inst:
        b = cum[i, d]
        b_tot = b[0:1] if d == 1 else b[c - 1:c]
        k_t[i, d] = (g_refs[d][i, :, GLA_KW:2 * GLA_KW].astype(F32) * jnp.exp(b_tot - b)).astype(BF16)
        dec[i, d] = jnp.exp(b_tot)
        if with_out:
            q = g_refs[d][i, :, 0:GLA_KW].astype(F32) * (GLA_DK ** -0.5)
            q_inter[i, d] = q * jnp.exp(b)
            q_intra[i, d] = q * jnp.exp(b - b_tot)

    att = {}
    if with_out:
        for i, d in inst:
            for h in range(GLA_HEADS):
                sl = slice((h // 2) * 2 * GLA_DK, (h // 2 + 1) * 2 * GLA_DK)
                mask = first if h % 2 == 0 else jnp.logical_not(first)
                qi = jnp.where(mask, q_intra[i, d][:, sl], 0.0).astype(BF16)
                att[i, d, h] = jnp.where(tri[d], _dot_nt(qi, k_t[i, d][:, sl]), 0.0).astype(BF16)

    for i, d in inst:
        for p in range(GLA_PAIRS):
            sl = slice(p * 2 * GLA_DK, (p + 1) * 2 * GLA_DK)
            st = st_ref[i, d, p]
            kt = k_t[i, d][:, sl]
            upd = []
            for hh in range(2):
                h = 2 * p + hh
                vh = g_refs[d][i, :, 2 * GLA_KW + h * GLA_DV:2 * GLA_KW + (h + 1) * GLA_DV]
                if with_out:
                    mask = first if hh == 0 else jnp.logical_not(first)
                    qe = jnp.where(mask, q_inter[i, d][:, sl], 0.0).astype(BF16)
                    o = _dot_nt(qe, st.astype(BF16)) + _dot(att[i, d, h], vh)
                    o_refs[d][i, :, h * GLA_DV:(h + 1) * GLA_DV] = o.astype(o_refs[d].dtype)
                upd.append(_dot_tn(vh, kt))
            st_ref[i, d, p] = st * dec[i, d][:, sl] + jnp.where(first_st, upd[0], upd[1])


def _gla(gla_in, lw, init, with_out):
    batch, n, _ = gla_in.shape
    c = GLA_CHUNK
    nch = n // c
    bb = GLA_BATCH_BLOCK
    const = lambda b, s: (0, 0)
    wspec = pl.BlockSpec((128, GLA_KW), const)
    bspec = pl.BlockSpec((1, GLA_KW), const)
    fwd = lambda b, s: (b, s, 0)
    bwd = lambda b, s: (b, nch - 1 - s, 0)
    st_shape = (batch, 2, GLA_PAIRS, GLA_DV, 2 * GLA_DK)
    st_spec = pl.BlockSpec((bb,) + st_shape[1:], lambda b, s: (b, 0, 0, 0, 0))
    in_specs = [pl.BlockSpec((bb, c, GLA_IN_W), fwd), pl.BlockSpec((bb, c, GLA_IN_W), bwd),
                wspec, bspec, wspec, bspec]
    args = [gla_in, gla_in, lw['w_af'], lw['b_af'], lw['w_ab'], lw['b_ab']]
    out_specs, out_shape, scratch = [], [], []
    if init is not None:
        in_specs.append(st_spec)
        args.append(init)
    if with_out:
        o_shape = jax.ShapeDtypeStruct((batch, n, GLA_WIDTH), BF16)
        out_specs += [pl.BlockSpec((bb, c, GLA_WIDTH), fwd), pl.BlockSpec((bb, c, GLA_WIDTH), bwd)]
        out_shape += [o_shape, o_shape]
    if init is None:
        out_specs.append(st_spec)
        out_shape.append(jax.ShapeDtypeStruct(st_shape, F32))
    else:
        scratch.append(pltpu.VMEM((bb,) + st_shape[1:], F32))
    return pl.pallas_call(
        functools.partial(_gla_kernel, bb=bb, has_init=init is not None, with_out=with_out),
        grid=(batch // bb, nch),
        in_specs=in_specs,
        out_specs=out_specs,
        out_shape=out_shape,
        scratch_shapes=scratch,
        compiler_params=pltpu.CompilerParams(dimension_semantics=("parallel", "arbitrary"),
                                             vmem_limit_bytes=VMEM_LIMIT),
        name="gla_scan",
    )(*args)


def _merge_kernel(x_ref, ss_ref, xb_ref, att_ref, pc_ref, pp_ref, pn_ref, of_ref, ob_ref,
                  wg_ref, gn_ref, post_ref, pw_ref, ps_ref, wbm_ref, wbp_ref, wbg_ref, wout_ref,
                  o_ref, ubuf):
    tm = x_ref.shape[1]
    j = pl.program_id(1)
    seq_len = tm * pl.num_programs(1)
    x = x_ref[0]
    ss = ss_ref[0]
    xb = xb_ref[0]

    branch_gates = _silu(_dot(xb, wg_ref[:, 0:1536]))
    merge_gates = _dot(xb, wg_ref[:, 1536:GATES_W])

    hal = POOL_HALO
    edge = 8
    ubuf[0:hal] = jnp.where(j > 0, pp_ref[0].astype(F32), 0.0)
    ubuf[hal:hal + tm] = pc_ref[0].astype(F32)
    ubuf[hal + tm:hal + tm + hal] = jnp.where(j < pl.num_programs(1) - 1, pn_ref[0].astype(F32), 0.0)
    r = lax.broadcasted_iota(jnp.int32, (edge, POOL_GROUP), 0)
    pooled = []
    for g, w in enumerate(POOL_WINDOWS):
        lanes = slice(g * POOL_GROUP, (g + 1) * POOL_GROUP)
        acc = None
        for d in range(-(w // 2), w // 2):
            piece = ubuf[hal + d:hal + d + tm, lanes]
            acc = piece if acc is None else acc + piece
        head_pos = j * tm + r
        tail_pos = j * tm + (tm - edge) + r
        head_cnt = (jnp.minimum(head_pos + w // 2, seq_len) - jnp.maximum(head_pos - w // 2, 0)).astype(F32)
        tail_cnt = (jnp.minimum(tail_pos + w // 2, seq_len) - jnp.maximum(tail_pos - w // 2, 0)).astype(F32)
        mean = jnp.concatenate([acc[0:edge] / head_cnt, acc[edge:tm - edge] * (1.0 / w),
                                acc[tm - edge:tm] / tail_cnt], axis=0)
        pooled.append((mean - ubuf[hal:hal + tm, lanes]).astype(BF16))

    y_mla = (att_ref[0].astype(F32) * branch_gates[:, 0:512]).astype(BF16)
    o = of_ref[0].astype(F32) + ob_ref[0].astype(F32)
    gn = gn_ref[...]
    o = jnp.concatenate([_rmsnorm(o[:, h * GLA_DV:(h + 1) * GLA_DV], gn) for h in range(GLA_HEADS)], axis=-1)
    y_gla = (o * branch_gates[:, 1024:1536]).astype(BF16)
    t_mla = _dot(y_mla, wbm_ref[...])
    t_gla = _dot(y_gla, wbg_ref[...])
    merged = _sigmoid(merge_gates[:, 0:D_MODEL]) * t_mla
    merged = merged + _sigmoid(merge_gates[:, 2 * D_MODEL:3 * D_MODEL]) * t_gla

    mixed = jnp.concatenate(
        [_dot(jnp.concatenate(pooled[2 * i:2 * i + 2], axis=-1), pw_ref[i]) for i in range(2)], axis=-1)
    y_pool = (mixed * ps_ref[...] * branch_gates[:, 512:1024]).astype(BF16)
    t_pool = _dot(y_pool, wbp_ref[...])

    merged = merged + _sigmoid(merge_gates[:, D_MODEL:2 * D_MODEL]) * t_pool
    out = _dot(merged.astype(BF16), wout_ref[...])
    o_ref[0] = x + ss[2:3] * _rmsnorm(out, post_ref[...])


def _merge(x, ss, xb, att, pool_x, o_f, o_b, lw, tm):
    batch, n, d = x.shape
    hb = tm // POOL_HALO
    n_halo = n // POOL_HALO
    per_batch_ss = ss.shape[0] == batch
    row = lambda b, j: (b, j, 0)
    const2 = lambda b, j: (0, 0)
    const3 = lambda b, j: (0, 0, 0)
    single = dict(pipeline_mode=pl.Buffered(1))
    return pl.pallas_call(
        _merge_kernel,
        grid=(batch, n // tm),
        in_specs=[pl.BlockSpec((1, tm, d), row),
                  pl.BlockSpec((1, 3, d), (lambda b, j: (b, 0, 0)) if per_batch_ss else (lambda b, j: (0, 0, 0))),
                  pl.BlockSpec((1, tm, d), row),
                  pl.BlockSpec((1, tm, MLA_WIDTH), row),
                  pl.BlockSpec((1, tm, POOL_WIDTH), row),
                  pl.BlockSpec((1, POOL_HALO, POOL_WIDTH), lambda b, j: (b, jnp.maximum(j * hb - 1, 0), 0)),
                  pl.BlockSpec((1, POOL_HALO, POOL_WIDTH),
                               lambda b, j: (b, jnp.minimum((j + 1) * hb, n_halo - 1), 0)),
                  pl.BlockSpec((1, tm, GLA_WIDTH), row),
                  pl.BlockSpec((1, tm, GLA_WIDTH), row),
                  pl.BlockSpec((d, GATES_W), const2, **single),
                  pl.BlockSpec((1, GLA_DV), const2),
                  pl.BlockSpec((1, d), const2),
                  pl.BlockSpec((2, 2 * POOL_GROUP, 2 * POOL_GROUP), const3),
                  pl.BlockSpec((1, POOL_WIDTH), const2),
                  pl.BlockSpec((MLA_WIDTH, d), const2, **single),
                  pl.BlockSpec((POOL_WIDTH, d), const2, **single),
                  pl.BlockSpec((GLA_WIDTH, d), const2, **single),
                  pl.BlockSpec((d, d), const2, **single)],
        out_specs=pl.BlockSpec((1, tm, d), row),
        out_shape=jax.ShapeDtypeStruct((batch, n, d), F32),
        scratch_shapes=[pltpu.VMEM((tm + 2 * POOL_HALO, POOL_WIDTH), F32)],
        compiler_params=pltpu.CompilerParams(dimension_semantics=("parallel", "parallel"),
                                             vmem_limit_bytes=VMEM_LIMIT),
        name="merge",
    )(x, ss, xb, att, pool_x, pool_x, pool_x, o_f, o_b, lw['w_gates'], lw['gla_norm'], lw['post_g'],
      lw['pool_w'], lw['pool_scale'], lw['w_bm'], lw['w_bp'], lw['w_bg'], lw['w_out'])


def _rot_cols(w):
    half = MLA_ROPE // 2
    quarter = half // 2

    def rot(v):
        return jnp.concatenate([-v[..., quarter:], v[..., :quarter]], axis=-1)

    return jnp.concatenate([rot(w[..., :half]), rot(w[..., half:])], axis=-1)


def _prep_w_in(w):
    offs = [0]
    for s in IN_SIZES:
        offs.append(offs[-1] + s)
    seg = {n: w[:, offs[i]:offs[i + 1]] for i, n in enumerate(IN_NAMES)}
    d = w.shape[0]
    z = lambda n: jnp.zeros((d, n), w.dtype)
    kr_pad = jnp.concatenate([z(MLA_NOPE), seg['mla_kr'], z(HEAD_PAD - MLA_NOPE - MLA_ROPE)], axis=1)
    kr_rot = jnp.concatenate([z(MLA_NOPE), _rot_cols(seg['mla_kr']), z(HEAD_PAD - MLA_NOPE - MLA_ROPE)], axis=1)
    gate_in = jnp.concatenate([seg['gla_af'], seg['gla_ab'], z(128 - 2 * GLA_GATE_RANK)], axis=1)
    w_heads = jnp.concatenate([seg['mla_q'], seg['mla_kv'], kr_pad, kr_rot,
                               seg['gla_q'], seg['gla_k'], seg['gla_v'], gate_in, seg['pool_x']], axis=1)
    w_gates = jnp.concatenate([seg['mla_gate'], seg['pool_gate'], seg['gla_gate'], seg['merge']], axis=1)
    return w_heads.astype(BF16), w_gates.astype(BF16)


def _prep_w_uq(w):
    r = w.shape[0]
    w3 = w.reshape(r, MLA_HEADS, MLA_NOPE + MLA_ROPE)
    nope, rope = w3[..., :MLA_NOPE], w3[..., MLA_NOPE:]
    pad = jnp.zeros((r, MLA_HEADS, HEAD_PAD - MLA_NOPE - MLA_ROPE), w.dtype)
    plain = jnp.concatenate([nope, rope, pad], axis=-1).reshape(r, MLA_HEADS * HEAD_PAD)
    rotated = jnp.concatenate([jnp.zeros_like(nope), _rot_cols(rope), pad], axis=-1).reshape(r, MLA_HEADS * HEAD_PAD)
    return jnp.concatenate([plain, rotated], axis=1).astype(BF16)


def _prep_w_ukv(w):
    r = w.shape[0]
    w3 = w.reshape(r, MLA_HEADS, MLA_NOPE + MLA_V)
    k_nope, v = w3[..., :MLA_NOPE], w3[..., MLA_NOPE:]
    zk = jnp.zeros((r, MLA_HEADS, HEAD_PAD - MLA_NOPE), w.dtype)
    k_pad = jnp.concatenate([k_nope, zk], axis=-1).reshape(r, MLA_HEADS * HEAD_PAD)
    zv = jnp.zeros_like(v)
    v_even = jnp.concatenate([v, zv], axis=-1)
    v_odd = jnp.concatenate([zv, v], axis=-1)
    odd = (jnp.arange(MLA_HEADS) % 2 == 1)[None, :, None]
    v_pad = jnp.where(odd, v_odd, v_even).reshape(r, MLA_HEADS * HEAD_PAD)
    return jnp.concatenate([k_pad, v_pad], axis=1).astype(BF16)


def _prep_gate_w(w2, slot):
    full = jnp.zeros((128, GLA_KW), w2.dtype)
    return full.at[slot * GLA_GATE_RANK:(slot + 1) * GLA_GATE_RANK].set(w2).astype(BF16)


def _prep_pool_w(pw):
    z = jnp.zeros((POOL_GROUP, POOL_GROUP), pw.dtype)
    blocks = [jnp.concatenate([jnp.concatenate([pw[2 * j], z], axis=1),
                               jnp.concatenate([z, pw[2 * j + 1]], axis=1)], axis=0) for j in range(2)]
    return jnp.stack(blocks).astype(BF16)


def _head_tables(cos, sin):
    n = cos.shape[0]
    scale = (MLA_NOPE + MLA_ROPE) ** -0.5 * 1.4426950408889634
    tail = jnp.zeros((n, HEAD_PAD - MLA_NOPE - MLA_ROPE), F32)
    cq = jnp.concatenate([jnp.ones((n, MLA_NOPE), F32), cos, tail], axis=1) * scale
    sq = jnp.concatenate([jnp.zeros((n, MLA_NOPE), F32), sin, tail], axis=1) * scale
    ck = jnp.concatenate([jnp.zeros((n, MLA_NOPE), F32), cos, tail], axis=1)
    sk = jnp.concatenate([jnp.zeros((n, MLA_NOPE), F32), sin, tail], axis=1)
    return cq, sq, ck, sk


def _rope_tables(lat_len, ctx_len):
    half = MLA_ROPE // 2
    rows = lat_len // GRID_W
    row = jnp.repeat(jnp.arange(rows), GRID_W).astype(F32)
    col = jnp.tile(jnp.arange(GRID_W), rows).astype(F32)
    inv = ROPE_BASE ** (-jnp.arange(0, half, 2, dtype=F32) / half)
    ang_r = row[:, None] * inv
    ang_c = col[:, None] * inv
    ang = jnp.concatenate([ang_r, ang_r, ang_c, ang_c], axis=-1)
    lat = _head_tables(jnp.cos(ang), jnp.sin(ang))
    ctx = _head_tables(jnp.ones((ctx_len, MLA_ROPE), F32), jnp.zeros((ctx_len, MLA_ROPE), F32))
    return lat, ctx


def _layer_weights(l, pre_norm, post_norm, w_in, mla_q_norm, mla_w_uq, mla_kv_norm, mla_w_ukv, pool_w, pool_scale,
                   gla_af_w2, gla_af_b, gla_ab_w2, gla_ab_b, gla_norm, w_branch_mla, w_branch_pool, w_branch_gla,
                   w_out):
    w_heads, w_gates = _prep_w_in(w_in[l])
    return dict(
        pre_g=pre_norm[l][None], post_g=post_norm[l][None], w_in=w_heads, w_gates=w_gates,
        q_norm=mla_q_norm[l][None], w_uq=_prep_w_uq(mla_w_uq[l]),
        kv_norm=mla_kv_norm[l][None], w_ukv=_prep_w_ukv(mla_w_ukv[l]),
        pool_w=_prep_pool_w(pool_w[l]), pool_scale=pool_scale[l][None],
        w_af=_prep_gate_w(gla_af_w2[l], 0), b_af=gla_af_b[l][None],
        w_ab=_prep_gate_w(gla_ab_w2[l], 1), b_ab=gla_ab_b[l][None],
        gla_norm=gla_norm[l][None],
        w_bm=w_branch_mla[l].astype(BF16), w_bp=w_branch_pool[l].astype(BF16),
        w_bg=w_branch_gla[l].astype(BF16), w_out=w_out[l].astype(BF16))


def kernel(x, c, ctx, c_ctx, mod_w, mod_b, pre_norm, post_norm, w_in, mla_q_norm, mla_w_uq, mla_kv_norm, mla_w_ukv, pool_w, pool_scale, gla_af_w2, gla_af_b, gla_ab_w2, gla_ab_b, gla_norm, w_branch_mla, w_branch_pool, w_branch_gla, w_out):
    batch, lat_len, d = x.shape
    ctx_len = ctx.shape[1]
    depth = mod_w.shape[0]
    ctx_tile = min(ctx_len, LATENT_TILE)
    assert d == D_MODEL and lat_len % LATENT_TILE == 0 and ctx_len % ctx_tile == 0
    assert lat_len % GRID_W == 0 and ctx_len % GLA_CHUNK == 0
    assert batch % GLA_BATCH_BLOCK == 0

    mp = -(-(batch + 1) // 8) * 8
    c_all = jnp.concatenate([c, c_ctx[None], jnp.zeros((mp - batch - 1, d), c.dtype)], axis=0)
    mod_all = _modulation(c_all, mod_w.astype(BF16), mod_b)
    tab_lat, tab_ctx = _rope_tables(lat_len, ctx_len)

    xc = ctx
    for l in range(depth):
        last = l == depth - 1
        lw = _layer_weights(l, pre_norm, post_norm, w_in, mla_q_norm, mla_w_uq, mla_kv_norm, mla_w_ukv, pool_w,
                            pool_scale, gla_af_w2, gla_af_b, gla_ab_w2, gla_ab_b, gla_norm, w_branch_mla,
                            w_branch_pool, w_branch_gla, w_out)
        ss = mod_all[l, :batch].reshape(batch, 3, d)
        ss_c = mod_all[l, batch].reshape(1, 3, d)

        qc, kc, vc, gla_c, pool_c, xb_c = _inproj(xc, ss_c, lw, tab_ctx, ctx_tile)
        q, k, v, gla_x, pool_x, xb = _inproj(x, ss, lw, tab_lat, LATENT_TILE)

        att = _attention(q, [(kc, vc), (k, v)])
        ctx_scan = _gla(gla_c, lw, None, with_out=not last)
        o_f, o_b = _gla(gla_x, lw, ctx_scan[-1], with_out=True)
        if not last:
            att_c = _attention(qc, [(kc, vc)])
            xc = _merge(xc, ss_c, xb_c, att_c, pool_c, ctx_scan[0], ctx_scan[1], lw, ctx_tile)
        x = _merge(x, ss, xb, att, pool_x, o_f, o_b, lw, LATENT_TILE)
    return x
```

```python
import functools

import numpy as np
import jax
import jax.numpy as jnp
from jax import lax
from jax.experimental import pallas as pl
from jax.experimental.pallas import tpu as pltpu

F32 = jnp.float32
BF16 = jnp.bfloat16

D_MODEL = 1024
NORM_EPS = 1e-6
GRID_W = 64
ROPE_BASE = 10000.0

MLA_HEADS = 8
MLA_Q_RANK = 256
MLA_KV_RANK = 128
MLA_NOPE = 64
MLA_ROPE = 32
MLA_V = 64
MLA_WIDTH = MLA_HEADS * MLA_V
HEAD_PAD = 128

POOL_WINDOWS = (2, 4, 8, 16)
POOL_WIDTH = 512
POOL_GROUP = 128
POOL_HALO = 16

GLA_HEADS = 4
GLA_DK = 64
GLA_DV = 128
GLA_KW = GLA_HEADS * GLA_DK
GLA_WIDTH = GLA_HEADS * GLA_DV
GLA_GATE_RANK = 16
GLA_TAU = 16.0
GLA_CHUNK = 128
GLA_PAIRS = GLA_HEADS // 2
GLA_Q_SCALE = GLA_DK ** -0.5

N_BRANCH = 3
IN_SIZES = (MLA_Q_RANK, MLA_KV_RANK, MLA_ROPE, MLA_WIDTH, POOL_WIDTH, POOL_WIDTH,
            GLA_KW, GLA_KW, GLA_WIDTH, GLA_GATE_RANK, GLA_GATE_RANK, GLA_WIDTH, N_BRANCH * D_MODEL)
IN_NAMES = ('mla_q', 'mla_kv', 'mla_kr', 'mla_gate', 'pool_x', 'pool_gate', 'gla_q', 'gla_k', 'gla_v',
            'gla_af', 'gla_ab', 'gla_gate', 'merge')

C_Q = 0
C_KV = C_Q + MLA_Q_RANK
C_KR = C_KV + MLA_KV_RANK
C_GLA = C_KR + 2 * HEAD_PAD
GLA_IN_W = 2 * GLA_KW + GLA_WIDTH + 128
C_POOL = C_GLA + GLA_IN_W
C_END = C_POOL + POOL_WIDTH
GATES_W = 3 * 512 + N_BRANCH * D_MODEL

LATENT_TILE = 512
ATTN_TILE = 512
SCORE_LOOKAHEAD = 1
GLA_BATCH_BLOCK = 8
VMEM_LIMIT = 56 * 1024 * 1024


def _sigmoid(x):
    return 0.5 * jnp.tanh(0.5 * x) + 0.5


def _silu(x):
    h = 0.5 * x
    return h * jnp.tanh(h) + h


def _rmsnorm(x, g):
    return x * lax.rsqrt(jnp.mean(x * x, axis=-1, keepdims=True) + NORM_EPS) * g


def _dot(a, b):
    return jnp.dot(a, b, preferred_element_type=F32)


def _dot_nt(a, b):
    return lax.dot_general(a, b, (((1,), (1,)), ((), ())), preferred_element_type=F32)


def _dot_tn(a, b):
    return lax.dot_general(a, b, (((0,), (0,)), ((), ())), preferred_element_type=F32)


def _modulated(x, ss, pre_g):
    return (_rmsnorm(x, pre_g) * (1.0 + ss[1:2]) + ss[0:1]).astype(BF16)


def _mod_kernel(c_ref, w_ref, b_ref, o_ref):
    c = c_ref[...]
    s = (c / (1.0 + jnp.exp(-c))).astype(BF16)
    o_ref[0] = _dot(s, w_ref[0]) + b_ref[0]


def _modulation(c_all, mod_w, mod_b):
    depth, d, n = mod_w.shape
    mp = c_all.shape[0]
    tn = 1024
    return pl.pallas_call(
        _mod_kernel,
        grid=(depth, n // tn),
        in_specs=[pl.BlockSpec((mp, d), lambda l, j: (0, 0)),
                  pl.BlockSpec((1, d, tn), lambda l, j: (l, 0, j)),
                  pl.BlockSpec((1, 1, tn), lambda l, j: (l, 0, j))],
        out_specs=pl.BlockSpec((1, mp, tn), lambda l, j: (l, 0, j)),
        out_shape=jax.ShapeDtypeStruct((depth, mp, n), F32),
        name="modulation",
    )(c_all, mod_w, mod_b.reshape(depth, 1, n))


def _inproj_kernel(x_ref, ss_ref, pre_ref, w_ref, qn_ref, wuq_ref, kvn_ref, wukv_ref,
                   cq_ref, sq_ref, ck_ref, sk_ref,
                   q_ref, k_ref, v_ref, gla_ref, pool_ref, xb_ref):
    xb = _modulated(x_ref[0], ss_ref[0], pre_ref[...])
    xb_ref[0] = xb
    gla_ref[0] =_dot(xb, w_ref[:, C_GLA:C_POOL]).astype(gla_ref.dtype)
    pool_ref[0] = _dot(xb, w_ref[:, C_POOL:C_END]).astype(pool_ref.dtype)

    hw = MLA_HEADS * HEAD_PAD
    zq = _dot(xb, w_ref[:, C_Q:C_KV])
    qq = _dot(_rmsnorm(zq, qn_ref[...]).astype(BF16), wuq_ref[...])
    cq, sq = cq_ref[...], sq_ref[...]
    for h in range(MLA_HEADS):
        lo = h * HEAD_PAD
        q_ref[0, h] = (qq[:, lo:lo + HEAD_PAD] * cq + qq[:, hw + lo:hw + lo + HEAD_PAD] * sq).astype(q_ref.dtype)

    zkv = _dot(xb, w_ref[:, C_KV:C_KR])
    zkr = _dot(xb, w_ref[:, C_KR:C_GLA])
    k_rope = zkr[:, :HEAD_PAD] * ck_ref[...] + zkr[:, HEAD_PAD:] * sk_ref[...]
    kv = _dot(_rmsnorm(zkv, kvn_ref[...]).astype(BF16), wukv_ref[...])
    for h in range(MLA_HEADS):
        lo = h * HEAD_PAD
        k_ref[0, h] = (kv[:, lo:lo + HEAD_PAD] + k_rope).astype(k_ref.dtype)
    for p in range(MLA_HEADS // 2):
        lo = hw + p * HEAD_PAD
        v_ref[0, p] = kv[:, lo:lo + HEAD_PAD].astype(v_ref.dtype)


def _inproj(x, ss, lw, tables, tm):
    batch, n, d = x.shape
    cq, sq, ck, sk = tables
    hw = MLA_HEADS * HEAD_PAD
    per_batch_ss = ss.shape[0] == batch
    const = lambda b, j: (0, 0)
    tab = pl.BlockSpec((tm, HEAD_PAD), lambda b, j: (j, 0))
    head_spec = pl.BlockSpec((1, MLA_HEADS, tm, HEAD_PAD), lambda b, j: (b, 0, j, 0))
    head_shape = jax.ShapeDtypeStruct((batch, MLA_HEADS, n, HEAD_PAD), BF16)
    row = lambda b, j: (b, j, 0)
    return pl.pallas_call(
        _inproj_kernel,
        grid=(batch, n // tm),
        in_specs=[pl.BlockSpec((1, tm, d), row),
                  pl.BlockSpec((1, 3, d), (lambda b, j: (b, 0, 0)) if per_batch_ss else (lambda b, j: (0, 0, 0))),
                  pl.BlockSpec((1, d), const),
                  pl.BlockSpec((d, C_END), const),
                  pl.BlockSpec((1, MLA_Q_RANK), const),
                  pl.BlockSpec((MLA_Q_RANK, 2 * hw), const),
                  pl.BlockSpec((1, MLA_KV_RANK), const),
                  pl.BlockSpec((MLA_KV_RANK, hw + MLA_WIDTH), const),
                  tab, tab, tab, tab],
        out_specs=[head_spec, head_spec,
                   pl.BlockSpec((1, MLA_HEADS // 2, tm, HEAD_PAD), lambda b, j: (b, 0, j, 0)),
                   pl.BlockSpec((1, tm, GLA_IN_W), row),
                   pl.BlockSpec((1, tm, POOL_WIDTH), row),
                   pl.BlockSpec((1, tm, d), row)],
        out_shape=[head_shape, head_shape,
                   jax.ShapeDtypeStruct((batch, MLA_HEADS // 2, n, HEAD_PAD), BF16),
                   jax.ShapeDtypeStruct((batch, n, GLA_IN_W), BF16),
                   jax.ShapeDtypeStruct((batch, n, POOL_WIDTH), BF16),
                   jax.ShapeDtypeStruct((batch, n, d), BF16)],
        compiler_params=pltpu.CompilerParams(dimension_semantics=("parallel", "parallel"),
                                             vmem_limit_bytes=VMEM_LIMIT),
        name="inproj",
    )(x, ss, lw['pre_g'], lw['w_in'], lw['q_norm'], lw['w_uq'], lw['kv_norm'], lw['w_ukv'], cq, sq, ck, sk)


def _attn_kernel(q_ref, *refs, n_kv):
    kv_refs, o_ref = refs[:2 * n_kv], refs[2 * n_kv]

    def scores(h):
        q = q_ref[0, h]
        return [_dot_nt(q, kv_refs[2 * i][0, h]) for i in range(n_kv)]

    def weighted_values(h, s):
        m = functools.reduce(jnp.maximum, [jnp.max(t, axis=-1, keepdims=True) for t in s])
        e = [jnp.exp2(t - m) for t in s]
        l = functools.reduce(jnp.add, [jnp.sum(t, axis=-1, keepdims=True) for t in e])
        pv = functools.reduce(jnp.add,
                              [_dot(e[i].astype(BF16), kv_refs[2 * i + 1][0, h // 2]) for i in range(n_kv)])
        return pv * (1.0 / l)

    pending = [scores(h) for h in range(SCORE_LOOKAHEAD)]
    even_lanes = lax.broadcasted_iota(jnp.int32, (q_ref.shape[2], HEAD_PAD), 1) < MLA_V
    even = None
    for h in range(MLA_HEADS):
        if h + SCORE_LOOKAHEAD < MLA_HEADS:
            pending.append(scores(h + SCORE_LOOKAHEAD))
        t = weighted_values(h, pending.pop(0))
        if h % 2 == 0:
            even = t
        else:
            pair = jnp.where(even_lanes, even, t)
            o_ref[0, :, (h // 2) * HEAD_PAD:(h // 2 + 1) * HEAD_PAD] = pair.astype(o_ref.dtype)


def _attention(q, kvs):
    batch, heads, n, hp = q.shape
    tq = min(ATTN_TILE, n)
    assert n % tq == 0
    in_specs = [pl.BlockSpec((1, heads, tq, hp), lambda b, j: (b, 0, j, 0))]
    args = [q]
    for k, v in kvs:
        in_specs += [pl.BlockSpec((1, heads, k.shape[2], hp), lambda b, j: (b, 0, 0, 0)),
                     pl.BlockSpec((1, heads // 2, v.shape[2], hp), lambda b, j: (b, 0, 0, 0))]
        args += [k, v]
    return pl.pallas_call(
        functools.partial(_attn_kernel, n_kv=len(kvs)),
        grid=(batch, n // tq),
        in_specs=in_specs,
        out_specs=pl.BlockSpec((1, tq, MLA_WIDTH), lambda b, j: (b, j, 0)),
        out_shape=jax.ShapeDtypeStruct((batch, n, MLA_WIDTH), BF16),
        compiler_params=pltpu.CompilerParams(dimension_semantics=("parallel", "arbitrary"),
                                             vmem_limit_bytes=VMEM_LIMIT),
        name="attention",
    )(*args)


def _gla_kernel(*refs, bb, has_init, with_out):
    g_refs = refs[0:2]
    wg_refs = (refs[2], refs[4])
    bg_refs = (refs[3], refs[5])
    pos = 6
    init_ref = None
    if has_init:
        init_ref = refs[pos]
        pos += 1
    o_refs = None
    if with_out:
        o_refs = (refs[pos], refs[pos + 1])
        pos += 2
    st_ref = refs[pos]

    @pl.when(pl.program_id(1) == 0)
    def _():
        if has_init:
            st_ref[...] = init_ref[...]
        else:
            st_ref[...] = jnp.zeros_like(st_ref)

    c = g_refs[0].shape[1]
    row = lax.broadcasted_iota(jnp.int32, (c, c), 0)
    col = lax.broadcasted_iota(jnp.int32, (c, c), 1)
    tri = (col <= row, col >= row)
    ones_tri = [jnp.where(t, 1.0, 0.0).astype(BF16) for t in tri]
    tri2 = [jnp.concatenate([t, t], axis=1) for t in tri]
    first = lax.broadcasted_iota(jnp.int32, (c, 2 * GLA_DK), 1) < GLA_DK
    first_st = lax.broadcasted_iota(jnp.int32, (GLA_DV, 2 * GLA_DK), 1) < GLA_DK
    k_t, dec, q_inter, q_intra = {}, {}, {}, {}

    def prepare(samples):
        for d in range(2):
            gate_in = jnp.concatenate([g_refs[d][i, :, 2 * GLA_KW + GLA_WIDTH:] for i in samples], axis=0)
            pre = _dot(gate_in, wg_refs[d][...]) + bg_refs[d][...]
            la = (jnp.minimum(pre, 0.0) - jnp.log1p(jnp.exp(-jnp.abs(pre)))) * (1.0 / GLA_TAU)
            la = jnp.concatenate([la[n * c:(n + 1) * c] for n in range(len(samples))], axis=1)
            hi = la.astype(BF16)
            r1 = la - hi.astype(F32)
            mid = r1.astype(BF16)
            lo = (r1 - mid.astype(F32)).astype(BF16)
            bsum = _dot(ones_tri[d], hi) + _dot(ones_tri[d], mid) + _dot(ones_tri[d], lo)
            for n, i in enumerate(samples):
                b = bsum[:, n * GLA_KW:(n + 1) * GLA_KW]
                b_tot = b[0:1] if d == 1 else b[c - 1:c]
                k_t[i, d] = (g_refs[d][i, :, GLA_KW:2 * GLA_KW].astype(F32) * jnp.exp(b_tot - b)).astype(BF16)
                dec[i, d] = jnp.exp(b_tot)
                if with_out:
                    q = g_refs[d][i, :, 0:GLA_KW].astype(F32)
                    q_inter[i, d] = q * jnp.exp(b)
                    q_intra[i, d] = q * jnp.exp(b - b_tot)

    def contract(samples):
        inst = [(i, d) for i in samples for d in range(2)]
        zero_v = jnp.zeros((c, GLA_DV), BF16)
        keys2, att2 = {}, {}
        for i, d in inst:
            for p in range(GLA_PAIRS):
                sl = slice(p * 2 * GLA_DK, (p + 1) * 2 * GLA_DK)
                kt = k_t[i, d][:, sl]
                zero_k = jnp.zeros_like(kt)
                keys2[i, d, p] = jnp.concatenate([jnp.where(first, kt, zero_k), jnp.where(first, zero_k, kt)], axis=0)
                if with_out:
                    a = _dot_nt(q_intra[i, d][:, sl].astype(BF16), keys2[i, d, p])
                    att2[i, d, p] = jnp.where(tri2[d], a, 0.0).astype(BF16)
        for i, d in inst:
            for p in range(GLA_PAIRS):
                sl = slice(p * 2 * GLA_DK, (p + 1) * 2 * GLA_DK)
                st = st_ref[i, d, p]
                lo = 2 * GLA_KW + 2 * p * GLA_DV
                v0 = g_refs[d][i, :, lo:lo + GLA_DV]
                v1 = g_refs[d][i, :, lo + GLA_DV:lo + 2 * GLA_DV]
                if with_out:
                    st2 = jnp.concatenate([jnp.where(first_st, st, 0.0), jnp.where(first_st, 0.0, st)],
                                          axis=0).astype(BF16)
                    v_diag = jnp.concatenate([jnp.concatenate([v0, zero_v], axis=1),
                                              jnp.concatenate([zero_v, v1], axis=1)], axis=0)
                    o = _dot_nt(q_inter[i, d][:, sl].astype(BF16), st2) + _dot(att2[i, d, p], v_diag)
                    o_refs[d][i, :, 2 * p * GLA_DV:(2 * p + 2) * GLA_DV] = o.astype(o_refs[d].dtype)
                upd = _dot_tn(jnp.concatenate([v0, v1], axis=0), keys2[i, d, p])
                st_ref[i, d, p] = st * dec[i, d][:, sl] + upd

    prepare(list(range(bb)))
    contract(list(range(bb)))


def _gla(gla_in, lw, init, with_out):
    batch, n, _ = gla_in.shape
    c = GLA_CHUNK
    nch = n // c
    bb = GLA_BATCH_BLOCK
    const = lambda b, s: (0, 0)
    wspec = pl.BlockSpec((128, GLA_KW), const)
    bspec = pl.BlockSpec((1, GLA_KW), const)
    fwd = lambda b, s: (b, s, 0)
    bwd = lambda b, s: (b, nch - 1 - s, 0)
    st_shape = (batch, 2, GLA_PAIRS, GLA_DV, 2 * GLA_DK)
    st_spec = pl.BlockSpec((bb,) + st_shape[1:], lambda b, s: (b, 0, 0, 0, 0))
    in_specs = [pl.BlockSpec((bb, c, GLA_IN_W), fwd), pl.BlockSpec((bb, c, GLA_IN_W), bwd),
                wspec, bspec, wspec, bspec]
    args = [gla_in, gla_in, lw['w_af'], lw['b_af'], lw['w_ab'], lw['b_ab']]
    out_specs, out_shape, scratch = [], [], []
    if init is not None:
        in_specs.append(st_spec)
        args.append(init)
    if with_out:
        o_shape = jax.ShapeDtypeStruct((batch, n, GLA_WIDTH), BF16)
        out_specs += [pl.BlockSpec((bb, c, GLA_WIDTH), fwd), pl.BlockSpec((bb, c, GLA_WIDTH), bwd)]
        out_shape += [o_shape, o_shape]
    if init is None:
        out_specs.append(st_spec)
        out_shape.append(jax.ShapeDtypeStruct(st_shape, F32))
    else:
        scratch.append(pltpu.VMEM((bb,) + st_shape[1:], F32))
    return pl.pallas_call(
        functools.partial(_gla_kernel, bb=bb, has_init=init is not None, with_out=with_out),
        grid=(batch // bb, nch),
        in_specs=in_specs,
        out_specs=out_specs,
        out_shape=out_shape,
        scratch_shapes=scratch,
        compiler_params=pltpu.CompilerParams(dimension_semantics=("parallel", "arbitrary"),
                                             vmem_limit_bytes=VMEM_LIMIT),
        name="gla_scan",
    )(*args)


def _merge_kernel(x_ref, ss_ref, xb_ref, att_ref, pc_ref, pp_ref, pn_ref, of_ref, ob_ref,
                  wg_ref, gn_ref, post_ref, pw_ref, ps_ref, wbm_ref, wbp_ref, wbg_ref, wout_ref,
                  o_ref, ubuf):
    tm = x_ref.shape[1]
    j = pl.program_id(1)
    seq_len = tm * pl.num_programs(1)
    x = x_ref[0]
    ss = ss_ref[0]
    xb = xb_ref[0]

    branch_gates = _silu(_dot(xb, wg_ref[:, 0:1536]))
    merge_gates = _dot(xb, wg_ref[:, 1536:GATES_W])

    hal = POOL_HALO
    edge = 8
    ubuf[0:hal] = jnp.where(j > 0, pp_ref[0].astype(F32), 0.0)
    ubuf[hal:hal + tm] = pc_ref[0].astype(F32)
    ubuf[hal + tm:hal + tm + hal] = jnp.where(j < pl.num_programs(1) - 1, pn_ref[0].astype(F32), 0.0)
    r = lax.broadcasted_iota(jnp.int32, (edge, POOL_GROUP), 0)
    pooled = []
    for g, w in enumerate(POOL_WINDOWS):
        lanes = slice(g * POOL_GROUP, (g + 1) * POOL_GROUP)
        acc = None
        for d in range(-(w // 2), w // 2):
            piece = ubuf[hal + d:hal + d + tm, lanes]
            acc = piece if acc is None else acc + piece
        head_pos = j * tm + r
        tail_pos = j * tm + (tm - edge) + r
        head_cnt = (jnp.minimum(head_pos + w // 2, seq_len) - jnp.maximum(head_pos - w // 2, 0)).astype(F32)
        tail_cnt = (jnp.minimum(tail_pos + w // 2, seq_len) - jnp.maximum(tail_pos - w // 2, 0)).astype(F32)
        mean = jnp.concatenate([acc[0:edge] / head_cnt, acc[edge:tm - edge] * (1.0 / w),
                                acc[tm - edge:tm] / tail_cnt], axis=0)
        pooled.append((mean - ubuf[hal:hal + tm, lanes]).astype(BF16))

    y_mla = (att_ref[0].astype(F32) * branch_gates[:, 0:512]).astype(BF16)
    o = of_ref[0].astype(F32) + ob_ref[0].astype(F32)
    gn = gn_ref[...]
    o = jnp.concatenate([_rmsnorm(o[:, h * GLA_DV:(h + 1) * GLA_DV], gn) for h in range(GLA_HEADS)], axis=-1)
    y_gla = (o * branch_gates[:, 1024:1536]).astype(BF16)
    t_mla = _dot(y_mla, wbm_ref[...])
    t_gla = _dot(y_gla, wbg_ref[...])
    merged = _sigmoid(merge_gates[:, 0:D_MODEL]) * t_mla
    merged = merged + _sigmoid(merge_gates[:, 2 * D_MODEL:3 * D_MODEL]) * t_gla

    mixed = jnp.concatenate(
        [_dot(jnp.concatenate(pooled[2 * i:2 * i + 2], axis=-1), pw_ref[i]) for i in range(2)], axis=-1)
    y_pool = (mixed * ps_ref[...] * branch_gates[:, 512:1024]).astype(BF16)
    t_pool = _dot(y_pool, wbp_ref[...])

    merged = merged + _sigmoid(merge_gates[:, D_MODEL:2 * D_MODEL]) * t_pool
    out = _dot(merged.astype(BF16), wout_ref[...])
    o_ref[0] = x + ss[2:3] * _rmsnorm(out, post_ref[...])


def _merge(x, ss, xb, att, pool_x, o_f, o_b, lw, tm):
    batch, n, d = x.shape
    hb = tm // POOL_HALO
    n_halo = n // POOL_HALO
    per_batch_ss = ss.shape[0] == batch
    row = lambda b, j: (b, j, 0)
    const2 = lambda b, j: (0, 0)
    const3 = lambda b, j: (0, 0, 0)
    single = dict(pipeline_mode=pl.Buffered(1))
    return pl.pallas_call(
        _merge_kernel,
        grid=(batch, n // tm),
        in_specs=[pl.BlockSpec((1, tm, d), row),
                  pl.BlockSpec((1, 3, d), (lambda b, j: (b, 0, 0)) if per_batch_ss else (lambda b, j: (0, 0, 0))),
                  pl.BlockSpec((1, tm, d), row),
                  pl.BlockSpec((1, tm, MLA_WIDTH), row),
                  pl.BlockSpec((1, tm, POOL_WIDTH), row),
                  pl.BlockSpec((1, POOL_HALO, POOL_WIDTH), lambda b, j: (b, jnp.maximum(j * hb - 1, 0), 0)),
                  pl.BlockSpec((1, POOL_HALO, POOL_WIDTH),
                               lambda b, j: (b, jnp.minimum((j + 1) * hb, n_halo - 1), 0)),
                  pl.BlockSpec((1, tm, GLA_WIDTH), row),
                  pl.BlockSpec((1, tm, GLA_WIDTH), row),
                  pl.BlockSpec((d, GATES_W), const2, **single),
                  pl.BlockSpec((1, GLA_DV), const2),
                  pl.BlockSpec((1, d), const2),
                  pl.BlockSpec((2, 2 * POOL_GROUP, 2 * POOL_GROUP), const3),
                  pl.BlockSpec((1, POOL_WIDTH), const2),
                  pl.BlockSpec((MLA_WIDTH, d), const2, **single),
                  pl.BlockSpec((POOL_WIDTH, d), const2, **single),
                  pl.BlockSpec((GLA_WIDTH, d), const2, **single),
                  pl.BlockSpec((d, d), const2, **single)],
        out_specs=pl.BlockSpec((1, tm, d), row),
        out_shape=jax.ShapeDtypeStruct((batch, n, d), F32),
        scratch_shapes=[pltpu.VMEM((tm + 2 * POOL_HALO, POOL_WIDTH), F32)],
        compiler_params=pltpu.CompilerParams(dimension_semantics=("parallel", "parallel"),
                                             vmem_limit_bytes=VMEM_LIMIT),
        name="merge",
    )(x, ss, xb, att, pool_x, pool_x, pool_x, o_f, o_b, lw['w_gates'], lw['gla_norm'], lw['post_g'],
      lw['pool_w'], lw['pool_scale'], lw['w_bm'], lw['w_bp'], lw['w_bg'], lw['w_out'])


def _rot_cols(w):
    half = MLA_ROPE // 2
    quarter = half // 2

    def rot(v):
        return jnp.concatenate([-v[..., quarter:], v[..., :quarter]], axis=-1)

    return jnp.concatenate([rot(w[..., :half]), rot(w[..., half:])], axis=-1)


def _prep_w_in(w):
    offs = [0]
    for s in IN_SIZES:
        offs.append(offs[-1] + s)
    seg = {n: w[:, offs[i]:offs[i + 1]] for i, n in enumerate(IN_NAMES)}
    d = w.shape[0]
    z = lambda n: jnp.zeros((d, n), w.dtype)
    kr_pad = jnp.concatenate([z(MLA_NOPE), seg['mla_kr'], z(HEAD_PAD - MLA_NOPE - MLA_ROPE)], axis=1)
    kr_rot = jnp.concatenate([z(MLA_NOPE), _rot_cols(seg['mla_kr']), z(HEAD_PAD - MLA_NOPE - MLA_ROPE)], axis=1)
    gate_in = jnp.concatenate([seg['gla_af'], seg['gla_ab'], z(128 - 2 * GLA_GATE_RANK)], axis=1)
    w_heads = jnp.concatenate([seg['mla_q'], seg['mla_kv'], kr_pad, kr_rot,
                               seg['gla_q'] * GLA_Q_SCALE, seg['gla_k'], seg['gla_v'], gate_in, seg['pool_x']],
                              axis=1)
    w_gates = jnp.concatenate([seg['mla_gate'], seg['pool_gate'], seg['gla_gate'], seg['merge']], axis=1)
    return w_heads.astype(BF16), w_gates.astype(BF16)


def _prep_w_uq(w):
    r = w.shape[0]
    w3 = w.reshape(r, MLA_HEADS, MLA_NOPE + MLA_ROPE)
    nope, rope = w3[..., :MLA_NOPE], w3[..., MLA_NOPE:]
    pad = jnp.zeros((r, MLA_HEADS, HEAD_PAD - MLA_NOPE - MLA_ROPE), w.dtype)
    plain = jnp.concatenate([nope, rope, pad], axis=-1).reshape(r, MLA_HEADS * HEAD_PAD)
    rotated = jnp.concatenate([jnp.zeros_like(nope), _rot_cols(rope), pad], axis=-1).reshape(r, MLA_HEADS * HEAD_PAD)
    return jnp.concatenate([plain, rotated], axis=1).astype(BF16)


def _prep_w_ukv(w):
    r = w.shape[0]
    w3 = w.reshape(r, MLA_HEADS, MLA_NOPE + MLA_V)
    k_nope, v = w3[..., :MLA_NOPE], w3[..., MLA_NOPE:]
    zk = jnp.zeros((r, MLA_HEADS, HEAD_PAD - MLA_NOPE), w.dtype)
    k_pad = jnp.concatenate([k_nope, zk], axis=-1).reshape(r, MLA_HEADS * HEAD_PAD)
    return jnp.concatenate([k_pad, v.reshape(r, MLA_WIDTH)], axis=1).astype(BF16)


def _prep_gate_w(w2, slot):
    full = jnp.zeros((128, GLA_KW), w2.dtype)
    return full.at[slot * GLA_GATE_RANK:(slot + 1) * GLA_GATE_RANK].set(w2).astype(BF16)


def _prep_pool_w(pw):
    z = jnp.zeros((POOL_GROUP, POOL_GROUP), pw.dtype)
    blocks = [jnp.concatenate([jnp.concatenate([pw[2 * j], z], axis=1),
                               jnp.concatenate([z, pw[2 * j + 1]], axis=1)], axis=0) for j in range(2)]
    return jnp.stack(blocks).astype(BF16)


def _head_tables(cos, sin):
    n = cos.shape[0]
    scale = (MLA_NOPE + MLA_ROPE) ** -0.5 * 1.4426950408889634
    tail = np.zeros((n, HEAD_PAD - MLA_NOPE - MLA_ROPE))
    cq = np.concatenate([np.ones((n, MLA_NOPE)), cos, tail], axis=1) * scale
    sq = np.concatenate([np.zeros((n, MLA_NOPE)), sin, tail], axis=1) * scale
    ck = np.concatenate([np.zeros((n, MLA_NOPE)), cos, tail], axis=1)
    sk = np.concatenate([np.zeros((n, MLA_NOPE)), sin, tail], axis=1)
    return tuple(jnp.asarray(t.astype(np.float32)) for t in (cq, sq, ck, sk))


def _rope_tables(lat_len, ctx_len):
    half = MLA_ROPE // 2
    rows = lat_len // GRID_W
    row = np.repeat(np.arange(rows), GRID_W).astype(np.float64)
    col = np.tile(np.arange(GRID_W), rows).astype(np.float64)
    inv = ROPE_BASE ** (-np.arange(0, half, 2, dtype=np.float64) / half)
    ang_r = row[:, None] * inv
    ang_c = col[:, None] * inv
    ang = np.concatenate([ang_r, ang_r, ang_c, ang_c], axis=-1)
    lat = _head_tables(np.cos(ang), np.sin(ang))
    ctx = _head_tables(np.ones((ctx_len, MLA_ROPE)), np.zeros((ctx_len, MLA_ROPE)))
    return lat, ctx


def _layer_weights(l, pre_norm, post_norm, w_in, mla_q_norm, mla_w_uq, mla_kv_norm, mla_w_ukv, pool_w, pool_scale,
                   gla_af_w2, gla_af_b, gla_ab_w2, gla_ab_b, gla_norm, w_branch_mla, w_branch_pool, w_branch_gla,
                   w_out):
    w_heads, w_gates = _prep_w_in(w_in[l])
    return dict(
        pre_g=pre_norm[l][None], post_g=post_norm[l][None], w_in=w_heads, w_gates=w_gates,
        q_norm=mla_q_norm[l][None], w_uq=_prep_w_uq(mla_w_uq[l]),
        kv_norm=mla_kv_norm[l][None], w_ukv=_prep_w_ukv(mla_w_ukv[l]),
        pool_w=_prep_pool_w(pool_w[l]), pool_scale=pool_scale[l][None],
        w_af=_prep_gate_w(gla_af_w2[l], 0), b_af=gla_af_b[l][None],
        w_ab=_prep_gate_w(gla_ab_w2[l], 1), b_ab=gla_ab_b[l][None],
        gla_norm=gla_norm[l][None],
        w_bm=w_branch_mla[l].astype(BF16), w_bp=w_branch_pool[l].astype(BF16),
        w_bg=w_branch_gla[l].astype(BF16), w_out=w_out[l].astype(BF16))


def kernel(x, c, ctx, c_ctx, mod_w, mod_b, pre_norm, post_norm, w_in, mla_q_norm, mla_w_uq, mla_kv_norm, mla_w_ukv, pool_w, pool_scale, gla_af_w2, gla_af_b, gla_ab_w2, gla_ab_b, gla_norm, w_branch_mla, w_branch_pool, w_branch_gla, w_out):
    batch, lat_len, d = x.shape
    ctx_len = ctx.shape[1]
    depth = mod_w.shape[0]
    ctx_tile = min(ctx_len, LATENT_TILE)
    assert d == D_MODEL and lat_len % LATENT_TILE == 0 and ctx_len % ctx_tile == 0
    assert lat_len % GRID_W == 0 and ctx_len % GLA_CHUNK == 0
    assert batch % GLA_BATCH_BLOCK == 0

    mp = -(-(batch + 1) // 8) * 8
    c_all = jnp.concatenate([c, c_ctx[None], jnp.zeros((mp - batch - 1, d), c.dtype)], axis=0)
    mod_all = _modulation(c_all, mod_w.astype(BF16), mod_b)
    tab_lat, tab_ctx = _rope_tables(lat_len, ctx_len)

    xc = ctx
    for l in range(depth):
        last = l == depth - 1
        lw = _layer_weights(l, pre_norm, post_norm, w_in, mla_q_norm, mla_w_uq, mla_kv_norm, mla_w_ukv, pool_w,
                            pool_scale, gla_af_w2, gla_af_b, gla_ab_w2, gla_ab_b, gla_norm, w_branch_mla,
                            w_branch_pool, w_branch_gla, w_out)
        ss = mod_all[l, :batch].reshape(batch, 3, d)
        ss_c = mod_all[l, batch].reshape(1, 3, d)

        qc, kc, vc, gla_c, pool_c, xb_c = _inproj(xc, ss_c, lw, tab_ctx, ctx_tile)
        q, k, v, gla_x, pool_x, xb = _inproj(x, ss, lw, tab_lat, LATENT_TILE)

        att = _attention(q, [(kc, vc), (k, v)])
        ctx_scan = _gla(gla_c, lw, None, with_out=not last)
        o_f, o_b = _gla(gla_x, lw, ctx_scan[-1], with_out=True)
        if not last:
            att_c = _attention(qc, [(kc, vc)])
            xc = _merge(xc, ss_c, xb_c, att_c, pool_c, ctx_scan[0], ctx_scan[1], lw, ctx_tile)
        x = _merge(x, ss, xb, att, pool_x, o_f, o_b, lw, LATENT_TILE)
    return x
```

```python
import functools

import numpy as np
import jax
import jax.numpy as jnp
from jax import lax
from jax.experimental import pallas as pl
from jax.experimental.pallas import tpu as pltpu

F32 = jnp.float32
BF16 = jnp.bfloat16

D_MODEL = 1024
NORM_EPS = 1e-6
GRID_W = 64
ROPE_BASE = 10000.0

MLA_HEADS = 8
MLA_Q_RANK = 256
MLA_KV_RANK = 128
MLA_NOPE = 64
MLA_ROPE = 32
MLA_V = 64
MLA_WIDTH = MLA_HEADS * MLA_V
HEAD_PAD = 128

POOL_WINDOWS = (2, 4, 8, 16)
POOL_WIDTH = 512
POOL_GROUP = 128
POOL_HALO = 16

GLA_HEADS = 4
GLA_DK = 64
GLA_DV = 128
GLA_KW = GLA_HEADS * GLA_DK
GLA_WIDTH = GLA_HEADS * GLA_DV
GLA_GATE_RANK = 16
GLA_TAU = 16.0
GLA_CHUNK = 128
GLA_PAIRS = GLA_HEADS // 2
GLA_Q_SCALE = GLA_DK ** -0.5

N_BRANCH = 3
IN_SIZES = (MLA_Q_RANK, MLA_KV_RANK, MLA_ROPE, MLA_WIDTH, POOL_WIDTH, POOL_WIDTH,
            GLA_KW, GLA_KW, GLA_WIDTH, GLA_GATE_RANK, GLA_GATE_RANK, GLA_WIDTH, N_BRANCH * D_MODEL)
IN_NAMES = ('mla_q', 'mla_kv', 'mla_kr', 'mla_gate', 'pool_x', 'pool_gate', 'gla_q', 'gla_k', 'gla_v',
            'gla_af', 'gla_ab', 'gla_gate', 'merge')

C_Q = 0
C_KV = C_Q + MLA_Q_RANK
C_KR = C_KV + MLA_KV_RANK
C_GLA = C_KR + HEAD_PAD
GLA_IN_W = 2 * GLA_KW + GLA_WIDTH + 128
C_POOL = C_GLA + GLA_IN_W
C_END = C_POOL + POOL_WIDTH
GATES_W = 3 * 512 + N_BRANCH * D_MODEL

LATENT_TILE = 512
ATTN_TILE = 512
SCORE_LOOKAHEAD = 1
GLA_BATCH_BLOCK = 8
VMEM_LIMIT = 56 * 1024 * 1024


def _silu_of_half(h):
    return h * jnp.tanh(h) + h


def _twice_sigmoid_of_half(h):
    return jnp.tanh(h) + 1.0


def _rmsnorm(x, g):
    return x * lax.rsqrt(jnp.mean(x * x, axis=-1, keepdims=True) + NORM_EPS) * g


def _dot(a, b):
    return jnp.dot(a, b, preferred_element_type=F32)


def _dot_nt(a, b):
    return lax.dot_general(a, b, (((1,), (1,)), ((), ())), preferred_element_type=F32)


def _dot_tn(a, b):
    return lax.dot_general(a, b, (((0,), (0,)), ((), ())), preferred_element_type=F32)


def _modulated(x, ss, pre_g):
    return (_rmsnorm(x, pre_g) * (1.0 + ss[1:2]) + ss[0:1]).astype(BF16)


def _mod_kernel(c_ref, w_ref, b_ref, o_ref):
    c = c_ref[...]
    s = (c / (1.0 + jnp.exp(-c))).astype(BF16)
    o_ref[0] = _dot(s, w_ref[0]) + b_ref[0]


def _modulation(c_all, mod_w, mod_b):
    depth, d, n = mod_w.shape
    mp = c_all.shape[0]
    tn = 1024
    return pl.pallas_call(
        _mod_kernel,
        grid=(depth, n // tn),
        in_specs=[pl.BlockSpec((mp, d), lambda l, j: (0, 0)),
                  pl.BlockSpec((1, d, tn), lambda l, j: (l, 0, j)),
                  pl.BlockSpec((1, 1, tn), lambda l, j: (l, 0, j))],
        out_specs=pl.BlockSpec((1, mp, tn), lambda l, j: (l, 0, j)),
        out_shape=jax.ShapeDtypeStruct((depth, mp, n), F32),
        name="modulation",
    )(c_all, mod_w, mod_b.reshape(depth, 1, n))


def _inproj_kernel(x_ref, ss_ref, pre_ref, w_ref, qn_ref, wuq_ref, kvn_ref, wukv_ref,
                   cq_ref, sq_ref, ck_ref, sk_ref,
                   q_ref, k_ref, v_ref, gla_ref, pool_ref, xb_ref):
    xb = _modulated(x_ref[0], ss_ref[0], pre_ref[...])
    xb_ref[0] = xb
    gla_ref[0] =_dot(xb, w_ref[:, C_GLA:C_POOL]).astype(gla_ref.dtype)
    pool_ref[0] = _dot(xb, w_ref[:, C_POOL:C_END]).astype(pool_ref.dtype)

    hw = MLA_HEADS * HEAD_PAD
    quarter = MLA_ROPE // 4
    lane = lax.broadcasted_iota(jnp.int32, (xb.shape[0], HEAD_PAD), 1)
    takes_negated = lane % (2 * quarter) < quarter

    def rotated(t):
        return jnp.where(takes_negated, -pltpu.roll(t, HEAD_PAD - quarter, axis=1), pltpu.roll(t, quarter, axis=1))

    zq = _dot(xb, w_ref[:, C_Q:C_KV])
    qq = _dot(_rmsnorm(zq, qn_ref[...]).astype(BF16), wuq_ref[...])
    cq, sq = cq_ref[...], sq_ref[...]
    for h in range(MLA_HEADS):
        qh = qq[:, h * HEAD_PAD:(h + 1) * HEAD_PAD]
        q_ref[0, h] = (qh * cq + rotated(qh) * sq).astype(q_ref.dtype)

    zkv = _dot(xb, w_ref[:, C_KV:C_KR])
    zkr = _dot(xb, w_ref[:, C_KR:C_GLA])
    k_rope = zkr * ck_ref[...] + rotated(zkr) * sk_ref[...]
    kv = _dot(_rmsnorm(zkv, kvn_ref[...]).astype(BF16), wukv_ref[...])
    for h in range(MLA_HEADS):
        lo = h * HEAD_PAD
        k_ref[0, h] = (kv[:, lo:lo + HEAD_PAD] + k_rope).astype(k_ref.dtype)
    for p in range(MLA_HEADS // 2):
        lo = hw + p * HEAD_PAD
        v_ref[0, p] = kv[:, lo:lo + HEAD_PAD].astype(v_ref.dtype)


def _inproj(x, ss, lw, tables, tm):
    batch, n, d = x.shape
    cq, sq, ck, sk = tables
    hw = MLA_HEADS * HEAD_PAD
    per_batch_ss = ss.shape[0] == batch
    const = lambda b, j: (0, 0)
    tab = pl.BlockSpec((tm, HEAD_PAD), lambda b, j: (j, 0))
    head_spec = pl.BlockSpec((1, MLA_HEADS, tm, HEAD_PAD), lambda b, j: (b, 0, j, 0))
    head_shape = jax.ShapeDtypeStruct((batch, MLA_HEADS, n, HEAD_PAD), BF16)
    row = lambda b, j: (b, j, 0)
    return pl.pallas_call(
        _inproj_kernel,
        grid=(batch, n // tm),
        in_specs=[pl.BlockSpec((1, tm, d), row),
                  pl.BlockSpec((1, 3, d), (lambda b, j: (b, 0, 0)) if per_batch_ss else (lambda b, j: (0, 0, 0))),
                  pl.BlockSpec((1, d), const),
                  pl.BlockSpec((d, C_END), const),
                  pl.BlockSpec((1, MLA_Q_RANK), const),
                  pl.BlockSpec((MLA_Q_RANK, hw), const),
                  pl.BlockSpec((1, MLA_KV_RANK), const),
                  pl.BlockSpec((MLA_KV_RANK, hw + MLA_WIDTH), const),
                  tab, tab, tab, tab],
        out_specs=[head_spec, head_spec,
                   pl.BlockSpec((1, MLA_HEADS // 2, tm, HEAD_PAD), lambda b, j: (b, 0, j, 0)),
                   pl.BlockSpec((1, tm, GLA_IN_W), row),
                   pl.BlockSpec((1, tm, POOL_WIDTH), row),
                   pl.BlockSpec((1, tm, d), row)],
        out_shape=[head_shape, head_shape,
                   jax.ShapeDtypeStruct((batch, MLA_HEADS // 2, n, HEAD_PAD), BF16),
                   jax.ShapeDtypeStruct((batch, n, GLA_IN_W), BF16),
                   jax.ShapeDtypeStruct((batch, n, POOL_WIDTH), BF16),
                   jax.ShapeDtypeStruct((batch, n, d), BF16)],
        compiler_params=pltpu.CompilerParams(dimension_semantics=("parallel", "parallel"),
                                             vmem_limit_bytes=VMEM_LIMIT),
        name="inproj",
    )(x, ss, lw['pre_g'], lw['w_in'], lw['q_norm'], lw['w_uq'], lw['kv_norm'], lw['w_ukv'], cq, sq, ck, sk)


def _attn_kernel(q_ref, *refs, n_kv):
    kv_refs, o_ref = refs[:2 * n_kv], refs[2 * n_kv]

    def scores(h):
        q = q_ref[0, h]
        return [_dot_nt(q, kv_refs[2 * i][0, h]) for i in range(n_kv)]

    def weighted_values(h, s):
        m = functools.reduce(jnp.maximum, [jnp.max(t, axis=-1, keepdims=True) for t in s])
        e = [jnp.exp2(t - m) for t in s]
        l = functools.reduce(jnp.add, [jnp.sum(t, axis=-1, keepdims=True) for t in e])
        pv = functools.reduce(jnp.add,
                              [_dot(e[i].astype(BF16), kv_refs[2 * i + 1][0, h // 2]) for i in range(n_kv)])
        return pv * (1.0 / l)

    pending = [scores(h) for h in range(SCORE_LOOKAHEAD)]
    even_lanes = lax.broadcasted_iota(jnp.int32, (q_ref.shape[2], HEAD_PAD), 1) < MLA_V
    even = None
    for h in range(MLA_HEADS):
        if h + SCORE_LOOKAHEAD < MLA_HEADS:
            pending.append(scores(h + SCORE_LOOKAHEAD))
        t = weighted_values(h, pending.pop(0))
        if h % 2 == 0:
            even = t
        else:
            pair = jnp.where(even_lanes, even, t)
            o_ref[0, :, (h // 2) * HEAD_PAD:(h // 2 + 1) * HEAD_PAD] = pair.astype(o_ref.dtype)


def _attention(q, kvs):
    batch, heads, n, hp = q.shape
    tq = min(ATTN_TILE, n)
    assert n % tq == 0
    in_specs = [pl.BlockSpec((1, heads, tq, hp), lambda b, j: (b, 0, j, 0))]
    args = [q]
    for k, v in kvs:
        in_specs += [pl.BlockSpec((1, heads, k.shape[2], hp), lambda b, j: (b, 0, 0, 0)),
                     pl.BlockSpec((1, heads // 2, v.shape[2], hp), lambda b, j: (b, 0, 0, 0))]
        args += [k, v]
    return pl.pallas_call(
        functools.partial(_attn_kernel, n_kv=len(kvs)),
        grid=(batch, n // tq),
        in_specs=in_specs,
        out_specs=pl.BlockSpec((1, tq, MLA_WIDTH), lambda b, j: (b, j, 0)),
        out_shape=jax.ShapeDtypeStruct((batch, n, MLA_WIDTH), BF16),
        compiler_params=pltpu.CompilerParams(dimension_semantics=("parallel", "arbitrary"),
                                             vmem_limit_bytes=VMEM_LIMIT),
        name="attention",
    )(*args)


def _gla_kernel(*refs, bb, has_init, with_out):
    g_refs = refs[0:2]
    wg_refs = (refs[2], refs[4])
    bg_refs = (refs[3], refs[5])
    pos = 6
    init_ref = None
    if has_init:
        init_ref = refs[pos]
        pos += 1
    o_refs = None
    if with_out:
        o_refs = (refs[pos], refs[pos + 1])
        pos += 2
    st_ref = refs[pos]

    @pl.when(pl.program_id(1) == 0)
    def _():
        if has_init:
            st_ref[...] = init_ref[...]
        else:
            st_ref[...] = jnp.zeros_like(st_ref)

    c = g_refs[0].shape[1]
    row = lax.broadcasted_iota(jnp.int32, (c, c), 0)
    col = lax.broadcasted_iota(jnp.int32, (c, c), 1)
    tri = (col <= row, col >= row)
    ones_tri = [jnp.where(t, 1.0, 0.0).astype(BF16) for t in tri]
    tri2 = [jnp.concatenate([t, t], axis=1) for t in tri]
    first = lax.broadcasted_iota(jnp.int32, (c, 2 * GLA_DK), 1) < GLA_DK
    first_st = lax.broadcasted_iota(jnp.int32, (GLA_DV, 2 * GLA_DK), 1) < GLA_DK
    k_t, dec, q_inter, q_intra = {}, {}, {}, {}

    def prepare(samples):
        for d in range(2):
            gate_in = jnp.concatenate([g_refs[d][i, :, 2 * GLA_KW + GLA_WIDTH:] for i in samples], axis=0)
            pre = _dot(gate_in, wg_refs[d][...]) + bg_refs[d][...]
            la = (jnp.minimum(pre, 0.0) - jnp.log1p(jnp.exp(-jnp.abs(pre)))) * (1.0 / GLA_TAU)
            la = jnp.concatenate([la[n * c:(n + 1) * c] for n in range(len(samples))], axis=1)
            hi = la.astype(BF16)
            r1 = la - hi.astype(F32)
            mid = r1.astype(BF16)
            lo = (r1 - mid.astype(F32)).astype(BF16)
            bsum = _dot(ones_tri[d], hi) + _dot(ones_tri[d], mid) + _dot(ones_tri[d], lo)
            for n, i in enumerate(samples):
                b = bsum[:, n * GLA_KW:(n + 1) * GLA_KW]
                b_tot = b[0:1] if d == 1 else b[c - 1:c]
                k_t[i, d] = (g_refs[d][i, :, GLA_KW:2 * GLA_KW].astype(F32) * jnp.exp(b_tot - b)).astype(BF16)
                dec[i, d] = jnp.exp(b_tot)
                if with_out:
                    q = g_refs[d][i, :, 0:GLA_KW].astype(F32)
                    q_inter[i, d] = q * jnp.exp(b)
                    q_intra[i, d] = q * jnp.exp(b - b_tot)

    def contract(samples):
        inst = [(i, d) for i in samples for d in range(2)]
        zero_v = jnp.zeros((c, GLA_DV), BF16)
        keys2, att2 = {}, {}
        for i, d in inst:
            for p in range(GLA_PAIRS):
                sl = slice(p * 2 * GLA_DK, (p + 1) * 2 * GLA_DK)
                kt = k_t[i, d][:, sl]
                zero_k = jnp.zeros_like(kt)
                keys2[i, d, p] = jnp.concatenate([jnp.where(first, kt, zero_k), jnp.where(first, zero_k, kt)], axis=0)
                if with_out:
                    a = _dot_nt(q_intra[i, d][:, sl].astype(BF16), keys2[i, d, p])
                    att2[i, d, p] = jnp.where(tri2[d], a, 0.0).astype(BF16)
        for i, d in inst:
            for p in range(GLA_PAIRS):
                sl = slice(p * 2 * GLA_DK, (p + 1) * 2 * GLA_DK)
                st = st_ref[i, d, p]
                lo = 2 * GLA_KW + 2 * p * GLA_DV
                v0 = g_refs[d][i, :, lo:lo + GLA_DV]
                v1 = g_refs[d][i, :, lo + GLA_DV:lo + 2 * GLA_DV]
                if with_out:
                    st2 = jnp.concatenate([jnp.where(first_st, st, 0.0), jnp.where(first_st, 0.0, st)],
                                          axis=0).astype(BF16)
                    v_diag = jnp.concatenate([jnp.concatenate([v0, zero_v], axis=1),
                                              jnp.concatenate([zero_v, v1], axis=1)], axis=0)
                    o = _dot_nt(q_inter[i, d][:, sl].astype(BF16), st2) + _dot(att2[i, d, p], v_diag)
                    o_refs[d][i, :, 2 * p * GLA_DV:(2 * p + 2) * GLA_DV] = o.astype(o_refs[d].dtype)
                upd = _dot_tn(jnp.concatenate([v0, v1], axis=0), keys2[i, d, p])
                st_ref[i, d, p] = st * dec[i, d][:, sl] + upd

    prepare(list(range(bb)))
    contract(list(range(bb)))


def _gla(gla_in, lw, init, with_out):
    batch, n, _ = gla_in.shape
    c = GLA_CHUNK
    nch = n // c
    bb = GLA_BATCH_BLOCK
    const = lambda b, s: (0, 0)
    wspec = pl.BlockSpec((128, GLA_KW), const)
    bspec = pl.BlockSpec((1, GLA_KW), const)
    fwd = lambda b, s: (b, s, 0)
    bwd = lambda b, s: (b, nch - 1 - s, 0)
    st_shape = (batch, 2, GLA_PAIRS, GLA_DV, 2 * GLA_DK)
    st_spec = pl.BlockSpec((bb,) + st_shape[1:], lambda b, s: (b, 0, 0, 0, 0))
    in_specs = [pl.BlockSpec((bb, c, GLA_IN_W), fwd), pl.BlockSpec((bb, c, GLA_IN_W), bwd),
                wspec, bspec, wspec, bspec]
    args = [gla_in, gla_in, lw['w_af'], lw['b_af'], lw['w_ab'], lw['b_ab']]
    out_specs, out_shape, scratch = [], [], []
    if init is not None:
        in_specs.append(st_spec)
        args.append(init)
    if with_out:
        o_shape = jax.ShapeDtypeStruct((batch, n, GLA_WIDTH), BF16)
        out_specs += [pl.BlockSpec((bb, c, GLA_WIDTH), fwd), pl.BlockSpec((bb, c, GLA_WIDTH), bwd)]
        out_shape += [o_shape, o_shape]
    if init is None:
        out_specs.append(st_spec)
        out_shape.append(jax.ShapeDtypeStruct(st_shape, F32))
    else:
        scratch.append(pltpu.VMEM((bb,) + st_shape[1:], F32))
    return pl.pallas_call(
        functools.partial(_gla_kernel, bb=bb, has_init=init is not None, with_out=with_out),
        grid=(batch // bb, nch),
        in_specs=in_specs,
        out_specs=out_specs,
        out_shape=out_shape,
        scratch_shapes=scratch,
        compiler_params=pltpu.CompilerParams(dimension_semantics=("parallel", "arbitrary"),
                                             vmem_limit_bytes=VMEM_LIMIT),
        name="gla_scan",
    )(*args)


def _merge_kernel(x_ref, ss_ref, xb_ref, att_ref, pc_ref, pp_ref, pn_ref, of_ref, ob_ref,
                  wg_ref, gn_ref, post_ref, pw_ref, ps_ref, wbm_ref, wbp_ref, wbg_ref, wout_ref,
                  o_ref, ubuf):
    tm = x_ref.shape[1]
    j = pl.program_id(1)
    seq_len = tm * pl.num_programs(1)
    x = x_ref[0]
    ss = ss_ref[0]
    xb = xb_ref[0]

    hal = POOL_HALO
    edge = 8
    n_ext = tm + 2 * hal
    no_halo = jnp.zeros_like(pp_ref[0])
    u_ext = jnp.concatenate([jnp.where(j > 0, pp_ref[0], no_halo), pc_ref[0],
                             jnp.where(j < pl.num_programs(1) - 1, pn_ref[0], no_halo)], axis=0)
    earlier = (lax.broadcasted_iota(jnp.int32, (n_ext, n_ext), 1)
               < lax.broadcasted_iota(jnp.int32, (n_ext, n_ext), 0))
    ubuf[...] = _dot(jnp.where(earlier, 1.0, 0.0).astype(BF16), u_ext)
    cur = pc_ref[0].astype(F32)
    r = lax.broadcasted_iota(jnp.int32, (edge, POOL_GROUP), 0)
    pooled = []
    for g, w in enumerate(POOL_WINDOWS):
        lanes = slice(g * POOL_GROUP, (g + 1) * POOL_GROUP)
        acc = ubuf[hal + w // 2:hal + w // 2 + tm, lanes] - ubuf[hal - w // 2:hal - w // 2 + tm, lanes]
        head_pos = j * tm + r
        tail_pos = j * tm + (tm - edge) + r
        head_cnt = (jnp.minimum(head_pos + w // 2, seq_len) - jnp.maximum(head_pos - w // 2, 0)).astype(F32)
        tail_cnt = (jnp.minimum(tail_pos + w // 2, seq_len) - jnp.maximum(tail_pos - w // 2, 0)).astype(F32)
        mean = jnp.concatenate([acc[0:edge] / head_cnt, acc[edge:tm - edge] * (1.0 / w),
                                acc[tm - edge:tm] / tail_cnt], axis=0)
        pooled.append((mean - cur[:, lanes]).astype(BF16))

    branch_gates = _silu_of_half(_dot(xb, wg_ref[:, 0:1536]))
    merge_gates = _dot(xb, wg_ref[:, 1536:GATES_W])

    y_mla = (att_ref[0].astype(F32) * branch_gates[:, 0:512]).astype(BF16)
    o = of_ref[0].astype(F32) + ob_ref[0].astype(F32)
    gn = gn_ref[...]
    o = jnp.concatenate([_rmsnorm(o[:, h * GLA_DV:(h + 1) * GLA_DV], gn) for h in range(GLA_HEADS)], axis=-1)
    y_gla = (o * branch_gates[:, 1024:1536]).astype(BF16)
    t_mla = _dot(y_mla, wbm_ref[...])
    t_gla = _dot(y_gla, wbg_ref[...])
    merged = _twice_sigmoid_of_half(merge_gates[:, 0:D_MODEL]) * t_mla
    merged = merged + _twice_sigmoid_of_half(merge_gates[:, 2 * D_MODEL:3 * D_MODEL]) * t_gla

    mixed = jnp.concatenate(
        [_dot(jnp.concatenate(pooled[2 * i:2 * i + 2], axis=-1), pw_ref[i]) for i in range(2)], axis=-1)
    y_pool = (mixed * ps_ref[...] * branch_gates[:, 512:1024]).astype(BF16)
    t_pool = _dot(y_pool, wbp_ref[...])

    merged = merged + _twice_sigmoid_of_half(merge_gates[:, D_MODEL:2 * D_MODEL]) * t_pool
    out = _dot(merged.astype(BF16), wout_ref[...])
    o_ref[0] = x + ss[2:3] * _rmsnorm(out, post_ref[...])


def _merge(x, ss, xb, att, pool_x, o_f, o_b, lw, tm):
    batch, n, d = x.shape
    hb = tm // POOL_HALO
    n_halo = n // POOL_HALO
    per_batch_ss = ss.shape[0] == batch
    row = lambda b, j: (b, j, 0)
    const2 = lambda b, j: (0, 0)
    const3 = lambda b, j: (0, 0, 0)
    single = dict(pipeline_mode=pl.Buffered(1))
    return pl.pallas_call(
        _merge_kernel,
        grid=(batch, n // tm),
        in_specs=[pl.BlockSpec((1, tm, d), row),
                  pl.BlockSpec((1, 3, d), (lambda b, j: (b, 0, 0)) if per_batch_ss else (lambda b, j: (0, 0, 0))),
                  pl.BlockSpec((1, tm, d), row),
                  pl.BlockSpec((1, tm, MLA_WIDTH), row),
                  pl.BlockSpec((1, tm, POOL_WIDTH), row),
                  pl.BlockSpec((1, POOL_HALO, POOL_WIDTH), lambda b, j: (b, jnp.maximum(j * hb - 1, 0), 0)),
                  pl.BlockSpec((1, POOL_HALO, POOL_WIDTH),
                               lambda b, j: (b, jnp.minimum((j + 1) * hb, n_halo - 1), 0)),
                  pl.BlockSpec((1, tm, GLA_WIDTH), row),
                  pl.BlockSpec((1, tm, GLA_WIDTH), row),
                  pl.BlockSpec((d, GATES_W), const2, **single),
                  pl.BlockSpec((1, GLA_DV), const2),
                  pl.BlockSpec((1, d), const2),
                  pl.BlockSpec((2, 2 * POOL_GROUP, 2 * POOL_GROUP), const3),
                  pl.BlockSpec((1, POOL_WIDTH), const2),
                  pl.BlockSpec((MLA_WIDTH, d), const2, **single),
                  pl.BlockSpec((POOL_WIDTH, d), const2, **single),
                  pl.BlockSpec((GLA_WIDTH, d), const2, **single),
                  pl.BlockSpec((d, d), const2, **single)],
        out_specs=pl.BlockSpec((1, tm, d), row),
        out_shape=jax.ShapeDtypeStruct((batch, n, d), F32),
        scratch_shapes=[pltpu.VMEM((tm + 2 * POOL_HALO, POOL_WIDTH), F32)],
        compiler_params=pltpu.CompilerParams(dimension_semantics=("parallel", "parallel"),
                                             vmem_limit_bytes=VMEM_LIMIT),
        name="merge",
    )(x, ss, xb, att, pool_x, pool_x, pool_x, o_f, o_b, lw['w_gates'], lw['gla_norm'], lw['post_g'],
      lw['pool_w'], lw['pool_scale'], lw['w_bm'], lw['w_bp'], lw['w_bg'], lw['w_out'])


def _prep_w_in(w):
    offs = [0]
    for s in IN_SIZES:
        offs.append(offs[-1] + s)
    seg = {n: w[:, offs[i]:offs[i + 1]] for i, n in enumerate(IN_NAMES)}
    d = w.shape[0]
    z = lambda n: jnp.zeros((d, n), w.dtype)
    kr_pad = jnp.concatenate([z(MLA_NOPE), seg['mla_kr'], z(HEAD_PAD - MLA_NOPE - MLA_ROPE)], axis=1)
    gate_in = jnp.concatenate([seg['gla_af'], seg['gla_ab'], z(128 - 2 * GLA_GATE_RANK)], axis=1)
    w_heads = jnp.concatenate([seg['mla_q'], seg['mla_kv'], kr_pad,
                               seg['gla_q'] * GLA_Q_SCALE, seg['gla_k'], seg['gla_v'], gate_in, seg['pool_x']],
                              axis=1)
    w_gates = jnp.concatenate([seg['mla_gate'], seg['pool_gate'], seg['gla_gate'], seg['merge']], axis=1) * 0.5
    return w_heads.astype(BF16), w_gates.astype(BF16)


def _prep_w_uq(w):
    r = w.shape[0]
    w3 = w.reshape(r, MLA_HEADS, MLA_NOPE + MLA_ROPE)
    nope, rope = w3[..., :MLA_NOPE], w3[..., MLA_NOPE:]
    pad = jnp.zeros((r, MLA_HEADS, HEAD_PAD - MLA_NOPE - MLA_ROPE), w.dtype)
    return jnp.concatenate([nope, rope, pad], axis=-1).reshape(r, MLA_HEADS * HEAD_PAD).astype(BF16)


def _prep_w_ukv(w):
    r = w.shape[0]
    w3 = w.reshape(r, MLA_HEADS, MLA_NOPE + MLA_V)
    k_nope, v = w3[..., :MLA_NOPE], w3[..., MLA_NOPE:]
    zk = jnp.zeros((r, MLA_HEADS, HEAD_PAD - MLA_NOPE), w.dtype)
    k_pad = jnp.concatenate([k_nope, zk], axis=-1).reshape(r, MLA_HEADS * HEAD_PAD)
    return jnp.concatenate([k_pad, v.reshape(r, MLA_WIDTH)], axis=1).astype(BF16)


def _prep_gate_w(w2, slot):
    full = jnp.zeros((128, GLA_KW), w2.dtype)
    return full.at[slot * GLA_GATE_RANK:(slot + 1) * GLA_GATE_RANK].set(w2).astype(BF16)


def _prep_pool_w(pw):
    z = jnp.zeros((POOL_GROUP, POOL_GROUP), pw.dtype)
    blocks = [jnp.concatenate([jnp.concatenate([pw[2 * j], z], axis=1),
                               jnp.concatenate([z, pw[2 * j + 1]], axis=1)], axis=0) for j in range(2)]
    return jnp.stack(blocks).astype(BF16)


def _head_tables(cos, sin):
    n = cos.shape[0]
    scale = (MLA_NOPE + MLA_ROPE) ** -0.5 * 1.4426950408889634
    tail = np.zeros((n, HEAD_PAD - MLA_NOPE - MLA_ROPE))
    cq = np.concatenate([np.ones((n, MLA_NOPE)), cos, tail], axis=1) * scale
    sq = np.concatenate([np.zeros((n, MLA_NOPE)), sin, tail], axis=1) * scale
    ck = np.concatenate([np.zeros((n, MLA_NOPE)), cos, tail], axis=1)
    sk = np.concatenate([np.zeros((n, MLA_NOPE)), sin, tail], axis=1)
    return tuple(jnp.asarray(t.astype(np.float32)) for t in (cq, sq, ck, sk))


def _rope_tables(lat_len, ctx_len):
    half = MLA_ROPE // 2
    rows = lat_len // GRID_W
    row = np.repeat(np.arange(rows), GRID_W).astype(np.float64)
    col = np.tile(np.arange(GRID_W), rows).astype(np.float64)
    inv = ROPE_BASE ** (-np.arange(0, half, 2, dtype=np.float64) / half)
    ang_r = row[:, None] * inv
    ang_c = col[:, None] * inv
    ang = np.concatenate([ang_r, ang_r, ang_c, ang_c], axis=-1)
    lat = _head_tables(np.cos(ang), np.sin(ang))
    ctx = _head_tables(np.ones((ctx_len, MLA_ROPE)), np.zeros((ctx_len, MLA_ROPE)))
    return lat, ctx


def _layer_weights(l, pre_norm, post_norm, w_in, mla_q_norm, mla_w_uq, mla_kv_norm, mla_w_ukv, pool_w, pool_scale,
                   gla_af_w2, gla_af_b, gla_ab_w2, gla_ab_b, gla_norm, w_branch_mla, w_branch_pool, w_branch_gla,
                   w_out):
    w_heads, w_gates = _prep_w_in(w_in[l])
    return dict(
        pre_g=pre_norm[l][None], post_g=post_norm[l][None], w_in=w_heads, w_gates=w_gates,
        q_norm=mla_q_norm[l][None], w_uq=_prep_w_uq(mla_w_uq[l]),
        kv_norm=mla_kv_norm[l][None], w_ukv=_prep_w_ukv(mla_w_ukv[l]),
        pool_w=_prep_pool_w(pool_w[l]), pool_scale=pool_scale[l][None],
        w_af=_prep_gate_w(gla_af_w2[l], 0), b_af=gla_af_b[l][None],
        w_ab=_prep_gate_w(gla_ab_w2[l], 1), b_ab=gla_ab_b[l][None],
        gla_norm=gla_norm[l][None],
        w_bm=(0.5 * w_branch_mla[l]).astype(BF16), w_bp=(0.5 * w_branch_pool[l]).astype(BF16),
        w_bg=(0.5 * w_branch_gla[l]).astype(BF16), w_out=w_out[l].astype(BF16))


def kernel(x, c, ctx, c_ctx, mod_w, mod_b, pre_norm, post_norm, w_in, mla_q_norm, mla_w_uq, mla_kv_norm, mla_w_ukv, pool_w, pool_scale, gla_af_w2, gla_af_b, gla_ab_w2, gla_ab_b, gla_norm, w_branch_mla, w_branch_pool, w_branch_gla, w_out):
    batch, lat_len, d = x.shape
    ctx_len = ctx.shape[1]
    depth = mod_w.shape[0]
    ctx_tile = min(ctx_len, LATENT_TILE)
    assert d == D_MODEL and lat_len % LATENT_TILE == 0 and ctx_len % ctx_tile == 0
    assert lat_len % GRID_W == 0 and ctx_len % GLA_CHUNK == 0
    assert batch % GLA_BATCH_BLOCK == 0

    mp = -(-(batch + 1) // 8) * 8
    c_all = jnp.concatenate([c, c_ctx[None], jnp.zeros((mp - batch - 1, d), c.dtype)], axis=0)
    mod_all = _modulation(c_all, mod_w.astype(BF16), mod_b)
    tab_lat, tab_ctx = _rope_tables(lat_len, ctx_len)

    xc = ctx
    for l in range(depth):
        last = l == depth - 1
        lw = _layer_weights(l, pre_norm, post_norm, w_in, mla_q_norm, mla_w_uq, mla_kv_norm, mla_w_ukv, pool_w,
                            pool_scale, gla_af_w2, gla_af_b, gla_ab_w2, gla_ab_b, gla_norm, w_branch_mla,
                            w_branch_pool, w_branch_gla, w_out)
        ss = mod_all[l, :batch].reshape(batch, 3, d)
        ss_c = mod_all[l, batch].reshape(1, 3, d)

        qc, kc, vc, gla_c, pool_c, xb_c = _inproj(xc, ss_c, lw, tab_ctx, ctx_tile)
        q, k, v, gla_x, pool_x, xb = _inproj(x, ss, lw, tab_lat, LATENT_TILE)

        att = _attention(q, [(kc, vc), (k, v)])
        ctx_scan = _gla(gla_c, lw, None, with_out=not last)
        o_f, o_b = _gla(gla_x, lw, ctx_scan[-1], with_out=True)
        if not last:
            att_c = _attention(qc, [(kc, vc)])
            xc = _merge(xc, ss_c, xb_c, att_c, pool_c, ctx_scan[0], ctx_scan[1], lw, ctx_tile)
        x = _merge(x, ss, xb, att, pool_x, o_f, o_b, lw, LATENT_TILE)
    return x
```

```python
import functools

import numpy as np
import jax
import jax.numpy as jnp
from jax import lax
from jax.experimental import pallas as pl
from jax.experimental.pallas import tpu as pltpu

F32 = jnp.float32
BF16 = jnp.bfloat16

D_MODEL = 1024
NORM_EPS = 1e-6
GRID_W = 64
ROPE_BASE = 10000.0

MLA_HEADS = 8
MLA_Q_RANK = 256
MLA_KV_RANK = 128
MLA_NOPE = 64
MLA_ROPE = 32
MLA_V = 64
MLA_WIDTH = MLA_HEADS * MLA_V
HEAD_PAD = 128

POOL_WINDOWS = (2, 4, 8, 16)
POOL_WIDTH = 512
POOL_GROUP = 128
POOL_HALO = 16

GLA_HEADS = 4
GLA_DK = 64
GLA_DV = 128
GLA_KW = GLA_HEADS * GLA_DK
GLA_WIDTH = GLA_HEADS * GLA_DV
GLA_GATE_RANK = 16
GLA_TAU = 16.0
GLA_CHUNK = 128
GLA_PAIRS = GLA_HEADS // 2
GLA_Q_SCALE = GLA_DK ** -0.5

N_BRANCH = 3
IN_SIZES = (MLA_Q_RANK, MLA_KV_RANK, MLA_ROPE, MLA_WIDTH, POOL_WIDTH, POOL_WIDTH,
            GLA_KW, GLA_KW, GLA_WIDTH, GLA_GATE_RANK, GLA_GATE_RANK, GLA_WIDTH, N_BRANCH * D_MODEL)
IN_NAMES = ('mla_q', 'mla_kv', 'mla_kr', 'mla_gate', 'pool_x', 'pool_gate', 'gla_q', 'gla_k', 'gla_v',
            'gla_af', 'gla_ab', 'gla_gate', 'merge')

C_Q = 0
C_KV = C_Q + MLA_Q_RANK
C_KR = C_KV + MLA_KV_RANK
C_GLA = C_KR + HEAD_PAD
GLA_IN_W = 2 * GLA_KW + GLA_WIDTH + 128
C_POOL = C_GLA + GLA_IN_W
C_END = C_POOL + POOL_WIDTH
GATES_W = 3 * 512 + N_BRANCH * D_MODEL

LATENT_TILE = 512
INPROJ_BLOCK_ROWS = 128
ATTN_TILE = 512
SCORE_LOOKAHEAD = 1
GLA_BATCH_BLOCK = 8
VMEM_LIMIT = 56 * 1024 * 1024


def _silu_of_half(h):
    return h * jnp.tanh(h) + h


def _twice_sigmoid_of_half(h):
    return jnp.tanh(h) + 1.0


def _rmsnorm(x, g):
    return x * lax.rsqrt(jnp.mean(x * x, axis=-1, keepdims=True) + NORM_EPS) * g


def _dot(a, b):
    return jnp.dot(a, b, preferred_element_type=F32)


def _dot_nt(a, b):
    return lax.dot_general(a, b, (((1,), (1,)), ((), ())), preferred_element_type=F32)


def _dot_tn(a, b):
    return lax.dot_general(a, b, (((0,), (0,)), ((), ())), preferred_element_type=F32)


def _modulated(x, ss, pre_g):
    return (_rmsnorm(x, pre_g) * (1.0 + ss[1:2]) + ss[0:1]).astype(BF16)


def _mod_kernel(c_ref, w_ref, b_ref, o_ref):
    c = c_ref[...]
    s = (c / (1.0 + jnp.exp(-c))).astype(BF16)
    o_ref[0] = _dot(s, w_ref[0]) + b_ref[0]


def _modulation(c_all, mod_w, mod_b):
    depth, d, n = mod_w.shape
    mp = c_all.shape[0]
    tn = 1024
    return pl.pallas_call(
        _mod_kernel,
        grid=(depth, n // tn),
        in_specs=[pl.BlockSpec((mp, d), lambda l, j: (0, 0)),
                  pl.BlockSpec((1, d, tn), lambda l, j: (l, 0, j)),
                  pl.BlockSpec((1, 1, tn), lambda l, j: (l, 0, j))],
        out_specs=pl.BlockSpec((1, mp, tn), lambda l, j: (l, 0, j)),
        out_shape=jax.ShapeDtypeStruct((depth, mp, n), F32),
        name="modulation",
    )(c_all, mod_w, mod_b.reshape(depth, 1, n))


def _inproj_kernel(x_ref, ss_ref, pre_ref, w_ref, qn_ref, wuq_ref, kvn_ref, wukv_ref,
                   cq_ref, sq_ref, ck_ref, sk_ref,
                   q_ref, k_ref, v_ref, gla_ref, pool_ref, xb_ref):
    tm = x_ref.shape[1]
    nb = min(tm, INPROJ_BLOCK_ROWS)
    n_blocks = tm // nb
    hw = MLA_HEADS * HEAD_PAD
    quarter = MLA_ROPE // 4
    lane = lax.broadcasted_iota(jnp.int32, (nb, HEAD_PAD), 1)
    takes_negated = lane % (2 * quarter) < quarter

    def rotated(t):
        return jnp.where(takes_negated, -pltpu.roll(t, HEAD_PAD - quarter, axis=1), pltpu.roll(t, quarter, axis=1))

    def project(r0):
        rows = slice(r0, r0 + nb)
        xb = _modulated(x_ref[0, rows, :], ss_ref[0], pre_ref[...])
        xb_ref[0, rows, :] = xb
        zq = _dot(xb, w_ref[:, C_Q:C_KV])
        zkv = _dot(xb, w_ref[:, C_KV:C_KR])
        zkr = _dot(xb, w_ref[:, C_KR:C_GLA])
        gla = _dot(xb, w_ref[:, C_GLA:C_POOL])
        pool = _dot(xb, w_ref[:, C_POOL:C_END])
        qq = _dot(_rmsnorm(zq, qn_ref[...]).astype(BF16), wuq_ref[...])
        kv = _dot(_rmsnorm(zkv, kvn_ref[...]).astype(BF16), wukv_ref[...])
        return zkr, gla, pool, qq, kv

    def emit(r0, zkr, gla, pool, qq, kv):
        rows = slice(r0, r0 + nb)
        gla_ref[0, rows, :] = gla.astype(gla_ref.dtype)
        pool_ref[0, rows, :] = pool.astype(pool_ref.dtype)
        cq, sq = cq_ref[rows, :], sq_ref[rows, :]
        for h in range(MLA_HEADS):
            qh = qq[:, h * HEAD_PAD:(h + 1) * HEAD_PAD]
            q_ref[0, h, rows, :] = (qh * cq + rotated(qh) * sq).astype(q_ref.dtype)
        k_rope = zkr * ck_ref[rows, :] + rotated(zkr) * sk_ref[rows, :]
        for h in range(MLA_HEADS):
            lo = h * HEAD_PAD
            k_ref[0, h, rows, :] = (kv[:, lo:lo + HEAD_PAD] + k_rope).astype(k_ref.dtype)
        for p in range(MLA_HEADS // 2):
            lo = hw + p * HEAD_PAD
            v_ref[0, p, rows, :] = kv[:, lo:lo + HEAD_PAD].astype(v_ref.dtype)

    pending = [project(0)]
    for n in range(n_blocks):
        if n + 1 < n_blocks:
            pending.append(project((n + 1) * nb))
        emit(n * nb, *pending.pop(0))


def _inproj(x, ss, lw, tables, tm):
    batch, n, d = x.shape
    cq, sq, ck, sk = tables
    hw = MLA_HEADS * HEAD_PAD
    per_batch_ss = ss.shape[0] == batch
    const = lambda b, j: (0, 0)
    tab = pl.BlockSpec((tm, HEAD_PAD), lambda b, j: (j, 0))
    head_spec = pl.BlockSpec((1, MLA_HEADS, tm, HEAD_PAD), lambda b, j: (b, 0, j, 0))
    head_shape = jax.ShapeDtypeStruct((batch, MLA_HEADS, n, HEAD_PAD), BF16)
    row = lambda b, j: (b, j, 0)
    return pl.pallas_call(
        _inproj_kernel,
        grid=(batch, n // tm),
        in_specs=[pl.BlockSpec((1, tm, d), row),
                  pl.BlockSpec((1, 3, d), (lambda b, j: (b, 0, 0)) if per_batch_ss else (lambda b, j: (0, 0, 0))),
                  pl.BlockSpec((1, d), const),
                  pl.BlockSpec((d, C_END), const),
                  pl.BlockSpec((1, MLA_Q_RANK), const),
                  pl.BlockSpec((MLA_Q_RANK, hw), const),
                  pl.BlockSpec((1, MLA_KV_RANK), const),
                  pl.BlockSpec((MLA_KV_RANK, hw + MLA_WIDTH), const),
                  tab, tab, tab, tab],
        out_specs=[head_spec, head_spec,
                   pl.BlockSpec((1, MLA_HEADS // 2, tm, HEAD_PAD), lambda b, j: (b, 0, j, 0)),
                   pl.BlockSpec((1, tm, GLA_IN_W), row),
                   pl.BlockSpec((1, tm, POOL_WIDTH), row),
                   pl.BlockSpec((1, tm, d), row)],
        out_shape=[head_shape, head_shape,
                   jax.ShapeDtypeStruct((batch, MLA_HEADS // 2, n, HEAD_PAD), BF16),
                   jax.ShapeDtypeStruct((batch, n, GLA_IN_W), BF16),
                   jax.ShapeDtypeStruct((batch, n, POOL_WIDTH), BF16),
                   jax.ShapeDtypeStruct((batch, n, d), BF16)],
        compiler_params=pltpu.CompilerParams(dimension_semantics=("parallel", "parallel"),
                                             vmem_limit_bytes=VMEM_LIMIT),
        name="inproj",
    )(x, ss, lw['pre_g'], lw['w_in'], lw['q_norm'], lw['w_uq'], lw['kv_norm'], lw['w_ukv'], cq, sq, ck, sk)


def _attn_kernel(q_ref, *refs, n_kv):
    kv_refs, o_ref = refs[:2 * n_kv], refs[2 * n_kv]

    def scores(h):
        q = q_ref[0, h]
        return [_dot_nt(q, kv_refs[2 * i][0, h]) for i in range(n_kv)]

    def weighted_values(h, s):
        m = functools.reduce(jnp.maximum, [jnp.max(t, axis=-1, keepdims=True) for t in s])
        e = [jnp.exp2(t - m) for t in s]
        l = functools.reduce(jnp.add, [jnp.sum(t, axis=-1, keepdims=True) for t in e])
        pv = functools.reduce(jnp.add,
                              [_dot(e[i].astype(BF16), kv_refs[2 * i + 1][0, h // 2]) for i in range(n_kv)])
        return pv * (1.0 / l)

    pending = [scores(h) for h in range(SCORE_LOOKAHEAD)]
    even_lanes = lax.broadcasted_iota(jnp.int32, (q_ref.shape[2], HEAD_PAD), 1) < MLA_V
    even = None
    for h in range(MLA_HEADS):
        if h + SCORE_LOOKAHEAD < MLA_HEADS:
            pending.append(scores(h + SCORE_LOOKAHEAD))
        t = weighted_values(h, pending.pop(0))
        if h % 2 == 0:
            even = t
        else:
            pair = jnp.where(even_lanes, even, t)
            o_ref[0, :, (h // 2) * HEAD_PAD:(h // 2 + 1) * HEAD_PAD] = pair.astype(o_ref.dtype)


def _attention(q, kvs):
    batch, heads, n, hp = q.shape
    tq = min(ATTN_TILE, n)
    assert n % tq == 0
    in_specs = [pl.BlockSpec((1, heads, tq, hp), lambda b, j: (b, 0, j, 0))]
    args = [q]
    for k, v in kvs:
        in_specs += [pl.BlockSpec((1, heads, k.shape[2], hp), lambda b, j: (b, 0, 0, 0)),
                     pl.BlockSpec((1, heads // 2, v.shape[2], hp), lambda b, j: (b, 0, 0, 0))]
        args += [k, v]
    return pl.pallas_call(
        functools.partial(_attn_kernel, n_kv=len(kvs)),
        grid=(batch, n // tq),
        in_specs=in_specs,
        out_specs=pl.BlockSpec((1, tq, MLA_WIDTH), lambda b, j: (b, j, 0)),
        out_shape=jax.ShapeDtypeStruct((batch, n, MLA_WIDTH), BF16),
        compiler_params=pltpu.CompilerParams(dimension_semantics=("parallel", "arbitrary"),
                                             vmem_limit_bytes=VMEM_LIMIT),
        name="attention",
    )(*args)


def _gla_kernel(*refs, bb, has_init, with_out):
    g_refs = refs[0:2]
    wg_refs = (refs[2], refs[4])
    bg_refs = (refs[3], refs[5])
    pos = 6
    init_ref = None
    if has_init:
        init_ref = refs[pos]
        pos += 1
    o_refs = None
    if with_out:
        o_refs = (refs[pos], refs[pos + 1])
        pos += 2
    st_ref = refs[pos]

    @pl.when(pl.program_id(1) == 0)
    def _():
        if has_init:
            st_ref[...] = init_ref[...]
        else:
            st_ref[...] = jnp.zeros_like(st_ref)

    c = g_refs[0].shape[1]
    row = lax.broadcasted_iota(jnp.int32, (c, c), 0)
    col = lax.broadcasted_iota(jnp.int32, (c, c), 1)
    tri = (col <= row, col >= row)
    ones_tri = [jnp.where(t, 1.0, 0.0).astype(BF16) for t in tri]
    tri2 = [jnp.concatenate([t, t], axis=1) for t in tri]
    first = lax.broadcasted_iota(jnp.int32, (c, 2 * GLA_DK), 1) < GLA_DK
    first_st = lax.broadcasted_iota(jnp.int32, (GLA_DV, 2 * GLA_DK), 1) < GLA_DK
    k_t, dec, q_inter, q_intra = {}, {}, {}, {}

    def prepare(samples):
        for d in range(2):
            gate_in = jnp.concatenate([g_refs[d][i, :, 2 * GLA_KW + GLA_WIDTH:] for i in samples], axis=0)
            pre = _dot(gate_in, wg_refs[d][...]) + bg_refs[d][...]
            la = (jnp.minimum(pre, 0.0) - jnp.log1p(jnp.exp(-jnp.abs(pre)))) * (1.0 / GLA_TAU)
            la = jnp.concatenate([la[n * c:(n + 1) * c] for n in range(len(samples))], axis=1)
            hi = la.astype(BF16)
            r1 = la - hi.astype(F32)
            mid = r1.astype(BF16)
            lo = (r1 - mid.astype(F32)).astype(BF16)
            bsum = _dot(ones_tri[d], hi) + _dot(ones_tri[d], mid) + _dot(ones_tri[d], lo)
            for n, i in enumerate(samples):
                b = bsum[:, n * GLA_KW:(n + 1) * GLA_KW]
                b_tot = b[0:1] if d == 1 else b[c - 1:c]
                k_t[i, d] = (g_refs[d][i, :, GLA_KW:2 * GLA_KW].astype(F32) * jnp.exp(b_tot - b)).astype(BF16)
                dec[i, d] = jnp.exp(b_tot)
                if with_out:
                    q = g_refs[d][i, :, 0:GLA_KW].astype(F32)
                    q_inter[i, d] = q * jnp.exp(b)
                    q_intra[i, d] = q * jnp.exp(b - b_tot)

    def contract(samples):
        inst = [(i, d) for i in samples for d in range(2)]
        zero_v = jnp.zeros((c, GLA_DV), BF16)
        keys2, att2 = {}, {}
        for i, d in inst:
            for p in range(GLA_PAIRS):
                sl = slice(p * 2 * GLA_DK, (p + 1) * 2 * GLA_DK)
                kt = k_t[i, d][:, sl]
                zero_k = jnp.zeros_like(kt)
                keys2[i, d, p] = jnp.concatenate([jnp.where(first, kt, zero_k), jnp.where(first, zero_k, kt)], axis=0)
                if with_out:
                    a = _dot_nt(q_intra[i, d][:, sl].astype(BF16), keys2[i, d, p])
                    att2[i, d, p] = jnp.where(tri2[d], a, 0.0).astype(BF16)
        for i, d in inst:
            for p in range(GLA_PAIRS):
                sl = slice(p * 2 * GLA_DK, (p + 1) * 2 * GLA_DK)
                st = st_ref[i, d, p]
                lo = 2 * GLA_KW + 2 * p * GLA_DV
                v0 = g_refs[d][i, :, lo:lo + GLA_DV]
                v1 = g_refs[d][i, :, lo + GLA_DV:lo + 2 * GLA_DV]
                if with_out:
                    st2 = jnp.concatenate([jnp.where(first_st, st, 0.0), jnp.where(first_st, 0.0, st)],
                                          axis=0).astype(BF16)
                    v_diag = jnp.concatenate([jnp.concatenate([v0, zero_v], axis=1),
                                              jnp.concatenate([zero_v, v1], axis=1)], axis=0)
                    o = _dot_nt(q_inter[i, d][:, sl].astype(BF16), st2) + _dot(att2[i, d, p], v_diag)
                    o_refs[d][i, :, 2 * p * GLA_DV:(2 * p + 2) * GLA_DV] = o.astype(o_refs[d].dtype)
                upd = _dot_tn(jnp.concatenate([v0, v1], axis=0), keys2[i, d, p])
                st_ref[i, d, p] = st * dec[i, d][:, sl] + upd

    prepare(list(range(bb)))
    contract(list(range(bb)))


def _gla(gla_in, lw, init, with_out):
    batch, n, _ = gla_in.shape
    c = GLA_CHUNK
    nch = n // c
    bb = GLA_BATCH_BLOCK
    const = lambda b, s: (0, 0)
    wspec = pl.BlockSpec((128, GLA_KW), const)
    bspec = pl.BlockSpec((1, GLA_KW), const)
    fwd = lambda b, s: (b, s, 0)
    bwd = lambda b, s: (b, nch - 1 - s, 0)
    st_shape = (batch, 2, GLA_PAIRS, GLA_DV, 2 * GLA_DK)
    st_spec = pl.BlockSpec((bb,) + st_shape[1:], lambda b, s: (b, 0, 0, 0, 0))
    in_specs = [pl.BlockSpec((bb, c, GLA_IN_W), fwd), pl.BlockSpec((bb, c, GLA_IN_W), bwd),
                wspec, bspec, wspec, bspec]
    args = [gla_in, gla_in, lw['w_af'], lw['b_af'], lw['w_ab'], lw['b_ab']]
    out_specs, out_shape, scratch = [], [], []
    if init is not None:
        in_specs.append(st_spec)
        args.append(init)
    if with_out:
        o_shape = jax.ShapeDtypeStruct((batch, n, GLA_WIDTH), BF16)
        out_specs += [pl.BlockSpec((bb, c, GLA_WIDTH), fwd), pl.BlockSpec((bb, c, GLA_WIDTH), bwd)]
        out_shape += [o_shape, o_shape]
    if init is None:
        out_specs.append(st_spec)
        out_shape.append(jax.ShapeDtypeStruct(st_shape, F32))
    else:
        scratch.append(pltpu.VMEM((bb,) + st_shape[1:], F32))
    return pl.pallas_call(
        functools.partial(_gla_kernel, bb=bb, has_init=init is not None, with_out=with_out),
        grid=(batch // bb, nch),
        in_specs=in_specs,
        out_specs=out_specs,
        out_shape=out_shape,
        scratch_shapes=scratch,
        compiler_params=pltpu.CompilerParams(dimension_semantics=("parallel", "arbitrary"),
                                             vmem_limit_bytes=VMEM_LIMIT),
        name="gla_scan",
    )(*args)


def _merge_kernel(x_ref, ss_ref, xb_ref, att_ref, pc_ref, pp_ref, pn_ref, of_ref, ob_ref,
                  wg_ref, gn_ref, post_ref, pw_ref, ps_ref, wbm_ref, wbp_ref, wbg_ref, wout_ref,
                  o_ref, ubuf):
    tm = x_ref.shape[1]
    j = pl.program_id(1)
    seq_len = tm * pl.num_programs(1)
    x = x_ref[0]
    ss = ss_ref[0]
    xb = xb_ref[0]

    hal = POOL_HALO
    edge = 8
    n_ext = tm + 2 * hal
    no_halo = jnp.zeros_like(pp_ref[0])
    u_ext = jnp.concatenate([jnp.where(j > 0, pp_ref[0], no_halo), pc_ref[0],
                             jnp.where(j < pl.num_programs(1) - 1, pn_ref[0], no_halo)], axis=0)
    earlier = (lax.broadcasted_iota(jnp.int32, (n_ext, n_ext), 1)
               < lax.broadcasted_iota(jnp.int32, (n_ext, n_ext), 0))
    ubuf[...] = _dot(jnp.where(earlier, 1.0, 0.0).astype(BF16), u_ext)
    cur = pc_ref[0].astype(F32)
    r = lax.broadcasted_iota(jnp.int32, (edge, POOL_GROUP), 0)
    pooled = []
    for g, w in enumerate(POOL_WINDOWS):
        lanes = slice(g * POOL_GROUP, (g + 1) * POOL_GROUP)
        acc = ubuf[hal + w // 2:hal + w // 2 + tm, lanes] - ubuf[hal - w // 2:hal - w // 2 + tm, lanes]
        head_pos = j * tm + r
        tail_pos = j * tm + (tm - edge) + r
        head_cnt = (jnp.minimum(head_pos + w // 2, seq_len) - jnp.maximum(head_pos - w // 2, 0)).astype(F32)
        tail_cnt = (jnp.minimum(tail_pos + w // 2, seq_len) - jnp.maximum(tail_pos - w // 2, 0)).astype(F32)
        mean = jnp.concatenate([acc[0:edge] / head_cnt, acc[edge:tm - edge] * (1.0 / w),
                                acc[tm - edge:tm] / tail_cnt], axis=0)
        pooled.append((mean - cur[:, lanes]).astype(BF16))

    branch_gates = _silu_of_half(_dot(xb, wg_ref[:, 0:1536]))
    merge_gates = _dot(xb, wg_ref[:, 1536:GATES_W])

    y_mla = (att_ref[0].astype(F32) * branch_gates[:, 0:512]).astype(BF16)
    o = of_ref[0].astype(F32) + ob_ref[0].astype(F32)
    gn = gn_ref[...]
    o = jnp.concatenate([_rmsnorm(o[:, h * GLA_DV:(h + 1) * GLA_DV], gn) for h in range(GLA_HEADS)], axis=-1)
    y_gla = (o * branch_gates[:, 1024:1536]).astype(BF16)
    t_mla = _dot(y_mla, wbm_ref[...])
    t_gla = _dot(y_gla, wbg_ref[...])
    merged = _twice_sigmoid_of_half(merge_gates[:, 0:D_MODEL]) * t_mla
    merged = merged + _twice_sigmoid_of_half(merge_gates[:, 2 * D_MODEL:3 * D_MODEL]) * t_gla

    mixed = jnp.concatenate(
        [_dot(jnp.concatenate(pooled[2 * i:2 * i + 2], axis=-1), pw_ref[i]) for i in range(2)], axis=-1)
    y_pool = (mixed * ps_ref[...] * branch_gates[:, 512:1024]).astype(BF16)
    t_pool = _dot(y_pool, wbp_ref[...])

    merged = merged + _twice_sigmoid_of_half(merge_gates[:, D_MODEL:2 * D_MODEL]) * t_pool
    out = _dot(merged.astype(BF16), wout_ref[...])
    o_ref[0] = x + ss[2:3] * _rmsnorm(out, post_ref[...])


def _merge(x, ss, xb, att, pool_x, o_f, o_b, lw, tm):
    batch, n, d = x.shape
    hb = tm // POOL_HALO
    n_halo = n // POOL_HALO
    per_batch_ss = ss.shape[0] == batch
    row = lambda b, j: (b, j, 0)
    const2 = lambda b, j: (0, 0)
    const3 = lambda b, j: (0, 0, 0)
    single = dict(pipeline_mode=pl.Buffered(1))
    return pl.pallas_call(
        _merge_kernel,
        grid=(batch, n // tm),
        in_specs=[pl.BlockSpec((1, tm, d), row),
                  pl.BlockSpec((1, 3, d), (lambda b, j: (b, 0, 0)) if per_batch_ss else (lambda b, j: (0, 0, 0))),
                  pl.BlockSpec((1, tm, d), row),
                  pl.BlockSpec((1, tm, MLA_WIDTH), row),
                  pl.BlockSpec((1, tm, POOL_WIDTH), row),
                  pl.BlockSpec((1, POOL_HALO, POOL_WIDTH), lambda b, j: (b, jnp.maximum(j * hb - 1, 0), 0)),
                  pl.BlockSpec((1, POOL_HALO, POOL_WIDTH),
                               lambda b, j: (b, jnp.minimum((j + 1) * hb, n_halo - 1), 0)),
                  pl.BlockSpec((1, tm, GLA_WIDTH), row),
                  pl.BlockSpec((1, tm, GLA_WIDTH), row),
                  pl.BlockSpec((d, GATES_W), const2, **single),
                  pl.BlockSpec((1, GLA_DV), const2),
                  pl.BlockSpec((1, d), const2),
                  pl.BlockSpec((2, 2 * POOL_GROUP, 2 * POOL_GROUP), const3),
                  pl.BlockSpec((1, POOL_WIDTH), const2),
                  pl.BlockSpec((MLA_WIDTH, d), const2, **single),
                  pl.BlockSpec((POOL_WIDTH, d), const2, **single),
                  pl.BlockSpec((GLA_WIDTH, d), const2, **single),
                  pl.BlockSpec((d, d), const2, **single)],
        out_specs=pl.BlockSpec((1, tm, d), row),
        out_shape=jax.ShapeDtypeStruct((batch, n, d), F32),
        scratch_shapes=[pltpu.VMEM((tm + 2 * POOL_HALO, POOL_WIDTH), F32)],
        compiler_params=pltpu.CompilerParams(dimension_semantics=("parallel", "parallel"),
                                             vmem_limit_bytes=VMEM_LIMIT),
        name="merge",
    )(x, ss, xb, att, pool_x, pool_x, pool_x, o_f, o_b, lw['w_gates'], lw['gla_norm'], lw['post_g'],
      lw['pool_w'], lw['pool_scale'], lw['w_bm'], lw['w_bp'], lw['w_bg'], lw['w_out'])


def _prep_w_in(w):
    offs = [0]
    for s in IN_SIZES:
        offs.append(offs[-1] + s)
    seg = {n: w[:, offs[i]:offs[i + 1]] for i, n in enumerate(IN_NAMES)}
    d = w.shape[0]
    z = lambda n: jnp.zeros((d, n), w.dtype)
    kr_pad = jnp.concatenate([z(MLA_NOPE), seg['mla_kr'], z(HEAD_PAD - MLA_NOPE - MLA_ROPE)], axis=1)
    gate_in = jnp.concatenate([seg['gla_af'], seg['gla_ab'], z(128 - 2 * GLA_GATE_RANK)], axis=1)
    w_heads = jnp.concatenate([seg['mla_q'], seg['mla_kv'], kr_pad,
                               seg['gla_q'] * GLA_Q_SCALE, seg['gla_k'], seg['gla_v'], gate_in, seg['pool_x']],
                              axis=1)
    w_gates = jnp.concatenate([seg['mla_gate'], seg['pool_gate'], seg['gla_gate'], seg['merge']], axis=1) * 0.5
    return w_heads.astype(BF16), w_gates.astype(BF16)


def _prep_w_uq(w):
    r = w.shape[0]
    w3 = w.reshape(r, MLA_HEADS, MLA_NOPE + MLA_ROPE)
    nope, rope = w3[..., :MLA_NOPE], w3[..., MLA_NOPE:]
    pad = jnp.zeros((r, MLA_HEADS, HEAD_PAD - MLA_NOPE - MLA_ROPE), w.dtype)
    return jnp.concatenate([nope, rope, pad], axis=-1).reshape(r, MLA_HEADS * HEAD_PAD).astype(BF16)


def _prep_w_ukv(w):
    r = w.shape[0]
    w3 = w.reshape(r, MLA_HEADS, MLA_NOPE + MLA_V)
    k_nope, v = w3[..., :MLA_NOPE], w3[..., MLA_NOPE:]
    zk = jnp.zeros((r, MLA_HEADS, HEAD_PAD - MLA_NOPE), w.dtype)
    k_pad = jnp.concatenate([k_nope, zk], axis=-1).reshape(r, MLA_HEADS * HEAD_PAD)
    return jnp.concatenate([k_pad, v.reshape(r, MLA_WIDTH)], axis=1).astype(BF16)


def _prep_gate_w(w2, slot):
    full = jnp.zeros((128, GLA_KW), w2.dtype)
    return full.at[slot * GLA_GATE_RANK:(slot + 1) * GLA_GATE_RANK].set(w2).astype(BF16)


def _prep_pool_w(pw):
    z = jnp.zeros((POOL_GROUP, POOL_GROUP), pw.dtype)
    blocks = [jnp.concatenate([jnp.concatenate([pw[2 * j], z], axis=1),
                               jnp.concatenate([z, pw[2 * j + 1]], axis=1)], axis=0) for j in range(2)]
    return jnp.stack(blocks).astype(BF16)


def _head_tables(cos, sin):
    n = cos.shape[0]
    scale = (MLA_NOPE + MLA_ROPE) ** -0.5 * 1.4426950408889634
    tail = np.zeros((n, HEAD_PAD - MLA_NOPE - MLA_ROPE))
    cq = np.concatenate([np.ones((n, MLA_NOPE)), cos, tail], axis=1) * scale
    sq = np.concatenate([np.zeros((n, MLA_NOPE)), sin, tail], axis=1) * scale
    ck = np.concatenate([np.zeros((n, MLA_NOPE)), cos, tail], axis=1)
    sk = np.concatenate([np.zeros((n, MLA_NOPE)), sin, tail], axis=1)
    return tuple(jnp.asarray(t.astype(np.float32)) for t in (cq, sq, ck, sk))


def _rope_tables(lat_len, ctx_len):
    half = MLA_ROPE // 2
    rows = lat_len // GRID_W
    row = np.repeat(np.arange(rows), GRID_W).astype(np.float64)
    col = np.tile(np.arange(GRID_W), rows).astype(np.float64)
    inv = ROPE_BASE ** (-np.arange(0, half, 2, dtype=np.float64) / half)
    ang_r = row[:, None] * inv
    ang_c = col[:, None] * inv
    ang = np.concatenate([ang_r, ang_r, ang_c, ang_c], axis=-1)
    lat = _head_tables(np.cos(ang), np.sin(ang))
    ctx = _head_tables(np.ones((ctx_len, MLA_ROPE)), np.zeros((ctx_len, MLA_ROPE)))
    return lat, ctx


def _layer_weights(l, pre_norm, post_norm, w_in, mla_q_norm, mla_w_uq, mla_kv_norm, mla_w_ukv, pool_w, pool_scale,
                   gla_af_w2, gla_af_b, gla_ab_w2, gla_ab_b, gla_norm, w_branch_mla, w_branch_pool, w_branch_gla,
                   w_out):
    w_heads, w_gates = _prep_w_in(w_in[l])
    return dict(
        pre_g=pre_norm[l][None], post_g=post_norm[l][None], w_in=w_heads, w_gates=w_gates,
        q_norm=mla_q_norm[l][None], w_uq=_prep_w_uq(mla_w_uq[l]),
        kv_norm=mla_kv_norm[l][None], w_ukv=_prep_w_ukv(mla_w_ukv[l]),
        pool_w=_prep_pool_w(pool_w[l]), pool_scale=pool_scale[l][None],
        w_af=_prep_gate_w(gla_af_w2[l], 0), b_af=gla_af_b[l][None],
        w_ab=_prep_gate_w(gla_ab_w2[l], 1), b_ab=gla_ab_b[l][None],
        gla_norm=gla_norm[l][None],
        w_bm=(0.5 * w_branch_mla[l]).astype(BF16), w_bp=(0.5 * w_branch_pool[l]).astype(BF16),
        w_bg=(0.5 * w_branch_gla[l]).astype(BF16), w_out=w_out[l].astype(BF16))


def kernel(x, c, ctx, c_ctx, mod_w, mod_b, pre_norm, post_norm, w_in, mla_q_norm, mla_w_uq, mla_kv_norm, mla_w_ukv, pool_w, pool_scale, gla_af_w2, gla_af_b, gla_ab_w2, gla_ab_b, gla_norm, w_branch_mla, w_branch_pool, w_branch_gla, w_out):
    batch, lat_len, d = x.shape
    ctx_len = ctx.shape[1]
    depth = mod_w.shape[0]
    ctx_tile = min(ctx_len, LATENT_TILE)
    assert d == D_MODEL and lat_len % LATENT_TILE == 0 and ctx_len % ctx_tile == 0
    assert lat_len % GRID_W == 0 and ctx_len % GLA_CHUNK == 0
    assert batch % GLA_BATCH_BLOCK == 0

    mp = -(-(batch + 1) // 8) * 8
    c_all = jnp.concatenate([c, c_ctx[None], jnp.zeros((mp - batch - 1, d), c.dtype)], axis=0)
    mod_all = _modulation(c_all, mod_w.astype(BF16), mod_b)
    tab_lat, tab_ctx = _rope_tables(lat_len, ctx_len)

    xc = ctx
    for l in range(depth):
        last = l == depth - 1
        lw = _layer_weights(l, pre_norm, post_norm, w_in, mla_q_norm, mla_w_uq, mla_kv_norm, mla_w_ukv, pool_w,
                            pool_scale, gla_af_w2, gla_af_b, gla_ab_w2, gla_ab_b, gla_norm, w_branch_mla,
                            w_branch_pool, w_branch_gla, w_out)
        ss = mod_all[l, :batch].reshape(batch, 3, d)
        ss_c = mod_all[l, batch].reshape(1, 3, d)

        qc, kc, vc, gla_c, pool_c, xb_c = _inproj(xc, ss_c, lw, tab_ctx, ctx_tile)
        q, k, v, gla_x, pool_x, xb = _inproj(x, ss, lw, tab_lat, LATENT_TILE)

        att = _attention(q, [(kc, vc), (k, v)])
        ctx_scan = _gla(gla_c, lw, None, with_out=not last)
        o_f, o_b = _gla(gla_x, lw, ctx_scan[-1], with_out=True)
        if not last:
            att_c = _attention(qc, [(kc, vc)])
            xc = _merge(xc, ss_c, xb_c, att_c, pool_c, ctx_scan[0], ctx_scan[1], lw, ctx_tile)
        x = _merge(x, ss, xb, att, pool_x, o_f, o_b, lw, LATENT_TILE)
    return x
```

```python
import functools

import numpy as np
import jax
import jax.numpy as jnp
from jax import lax
from jax.experimental import pallas as pl
from jax.experimental.pallas import tpu as pltpu

F32 = jnp.float32
BF16 = jnp.bfloat16

D_MODEL = 1024
NORM_EPS = 1e-6
GRID_W = 64
ROPE_BASE = 10000.0

MLA_HEADS = 8
MLA_Q_RANK = 256
MLA_KV_RANK = 128
MLA_NOPE = 64
MLA_ROPE = 32
MLA_V = 64
MLA_WIDTH = MLA_HEADS * MLA_V
HEAD_PAD = 128

POOL_WINDOWS = (2, 4, 8, 16)
POOL_WIDTH = 512
POOL_GROUP = 128
POOL_HALO = 16
POOL_EDGE = max(POOL_WINDOWS) // 2
POOL_PREFIX_BLOCK = 256

GLA_HEADS = 4
GLA_DK = 64
GLA_DV = 128
GLA_KW = GLA_HEADS * GLA_DK
GLA_WIDTH = GLA_HEADS * GLA_DV
GLA_GATE_RANK = 16
GLA_TAU = 16.0
GLA_CHUNK = 128
GLA_PAIRS = GLA_HEADS // 2
GLA_Q_SCALE = GLA_DK ** -0.5

N_BRANCH = 3
IN_SIZES = (MLA_Q_RANK, MLA_KV_RANK, MLA_ROPE, MLA_WIDTH, POOL_WIDTH, POOL_WIDTH,
            GLA_KW, GLA_KW, GLA_WIDTH, GLA_GATE_RANK, GLA_GATE_RANK, GLA_WIDTH, N_BRANCH * D_MODEL)
IN_NAMES = ('mla_q', 'mla_kv', 'mla_kr', 'mla_gate', 'pool_x', 'pool_gate', 'gla_q', 'gla_k', 'gla_v',
            'gla_af', 'gla_ab', 'gla_gate', 'merge')

C_Q = 0
C_KV = C_Q + MLA_Q_RANK
C_KR = C_KV + MLA_KV_RANK
C_GLA = C_KR + HEAD_PAD
GLA_GATE_PAD = 128
GLA_IN_W = 2 * GLA_KW + GLA_WIDTH + GLA_GATE_PAD
C_POOL = C_GLA + GLA_IN_W
C_END = C_POOL + POOL_WIDTH
BRANCH_W = 512
BRANCH_GATES_W = N_BRANCH * BRANCH_W
GATES_W = BRANCH_GATES_W + N_BRANCH * D_MODEL

LATENT_TILE = 512
INPROJ_BLOCK_ROWS = 128
ATTN_TILE = 512
SCORE_LOOKAHEAD = 1
GLA_BATCH_BLOCK = 8
VMEM_LIMIT = 56 * 1024 * 1024


def _silu_of_half(h):
    return h * jnp.tanh(h) + h


def _twice_sigmoid_of_half(h):
    return jnp.tanh(h) + 1.0


def _rmsnorm(x, g):
    return x * lax.rsqrt(jnp.mean(x * x, axis=-1, keepdims=True) + NORM_EPS) * g


def _dot(a, b):
    return jnp.dot(a, b, preferred_element_type=F32)


def _dot_nt(a, b):
    return lax.dot_general(a, b, (((1,), (1,)), ((), ())), preferred_element_type=F32)


def _dot_tn(a, b):
    return lax.dot_general(a, b, (((0,), (0,)), ((), ())), preferred_element_type=F32)


def _modulated(x, ss, pre_g):
    return (_rmsnorm(x, pre_g) * (1.0 + ss[1:2]) + ss[0:1]).astype(BF16)


def _mod_kernel(c_ref, w_ref, b_ref, o_ref):
    c = c_ref[...]
    s = (c / (1.0 + jnp.exp(-c))).astype(BF16)
    o_ref[0] = _dot(s, w_ref[0]) + b_ref[0]


def _modulation(c_all, mod_w, mod_b):
    depth, d, n = mod_w.shape
    mp = c_all.shape[0]
    tn = 1024
    return pl.pallas_call(
        _mod_kernel,
        grid=(depth, n // tn),
        in_specs=[pl.BlockSpec((mp, d), lambda l, j: (0, 0)),
                  pl.BlockSpec((1, d, tn), lambda l, j: (l, 0, j)),
                  pl.BlockSpec((1, 1, tn), lambda l, j: (l, 0, j))],
        out_specs=pl.BlockSpec((1, mp, tn), lambda l, j: (l, 0, j)),
        out_shape=jax.ShapeDtypeStruct((depth, mp, n), F32),
        name="modulation",
    )(c_all, mod_w, mod_b.reshape(depth, 1, n))


def _inproj_kernel(x_ref, ss_ref, pre_ref, w_ref, qn_ref, wuq_ref, kvn_ref, wukv_ref,
                   cq_ref, sq_ref, ck_ref, sk_ref,
                   q_ref, k_ref, v_ref, gla_ref, pool_ref, xb_ref):
    tm = x_ref.shape[1]
    nb = min(tm, INPROJ_BLOCK_ROWS)
    n_blocks = tm // nb
    hw = MLA_HEADS * HEAD_PAD
    quarter = MLA_ROPE // 4
    lane = lax.broadcasted_iota(jnp.int32, (nb, HEAD_PAD), 1)
    takes_negated = lane % (2 * quarter) < quarter

    def rotated(t):
        return jnp.where(takes_negated, -pltpu.roll(t, HEAD_PAD - quarter, axis=1), pltpu.roll(t, quarter, axis=1))

    def project(r0):
        rows = slice(r0, r0 + nb)
        xb = _modulated(x_ref[0, rows, :], ss_ref[0], pre_ref[...])
        xb_ref[0, rows, :] = xb
        zq = _dot(xb, w_ref[:, C_Q:C_KV])
        zkv = _dot(xb, w_ref[:, C_KV:C_KR])
        zkr = _dot(xb, w_ref[:, C_KR:C_GLA])
        gla = _dot(xb, w_ref[:, C_GLA:C_POOL])
        pool = _dot(xb, w_ref[:, C_POOL:C_END])
        qq = _dot(_rmsnorm(zq, qn_ref[...]).astype(BF16), wuq_ref[...])
        kv = _dot(_rmsnorm(zkv, kvn_ref[...]).astype(BF16), wukv_ref[...])
        return zkr, gla, pool, qq, kv

    def emit(r0, zkr, gla, pool, qq, kv):
        rows = slice(r0, r0 + nb)
        gla_ref[0, rows, :] = gla.astype(gla_ref.dtype)
        pool_ref[0, rows, :] = pool.astype(pool_ref.dtype)
        cq, sq = cq_ref[rows, :], sq_ref[rows, :]
        for h in range(MLA_HEADS):
            qh = qq[:, h * HEAD_PAD:(h + 1) * HEAD_PAD]
            q_ref[0, h, rows, :] = (qh * cq + rotated(qh) * sq).astype(q_ref.dtype)
        k_rope = zkr * ck_ref[rows, :] + rotated(zkr) * sk_ref[rows, :]
        for h in range(MLA_HEADS):
            lo = h * HEAD_PAD
            k_ref[0, h, rows, :] = (kv[:, lo:lo + HEAD_PAD] + k_rope).astype(k_ref.dtype)
        for p in range(MLA_HEADS // 2):
            lo = hw + p * HEAD_PAD
            v_ref[0, p, rows, :] = kv[:, lo:lo + HEAD_PAD].astype(v_ref.dtype)

    pending = [project(0)]
    for n in range(n_blocks):
        if n + 1 < n_blocks:
            pending.append(project((n + 1) * nb))
        emit(n * nb, *pending.pop(0))


def _inproj(x, ss, lw, tables, tm):
    batch, n, d = x.shape
    cq, sq, ck, sk = tables
    hw = MLA_HEADS * HEAD_PAD
    per_batch_ss = ss.shape[0] == batch
    const = lambda b, j: (0, 0)
    tab = pl.BlockSpec((tm, HEAD_PAD), lambda b, j: (j, 0))
    head_spec = pl.BlockSpec((1, MLA_HEADS, tm, HEAD_PAD), lambda b, j: (b, 0, j, 0))
    head_shape = jax.ShapeDtypeStruct((batch, MLA_HEADS, n, HEAD_PAD), BF16)
    row = lambda b, j: (b, j, 0)
    return pl.pallas_call(
        _inproj_kernel,
        grid=(batch, n // tm),
        in_specs=[pl.BlockSpec((1, tm, d), row),
                  pl.BlockSpec((1, 3, d), (lambda b, j: (b, 0, 0)) if per_batch_ss else (lambda b, j: (0, 0, 0))),
                  pl.BlockSpec((1, d), const),
                  pl.BlockSpec((d, C_END), const),
                  pl.BlockSpec((1, MLA_Q_RANK), const),
                  pl.BlockSpec((MLA_Q_RANK, hw), const),
                  pl.BlockSpec((1, MLA_KV_RANK), const),
                  pl.BlockSpec((MLA_KV_RANK, hw + MLA_WIDTH), const),
                  tab, tab, tab, tab],
        out_specs=[head_spec, head_spec,
                   pl.BlockSpec((1, MLA_HEADS // 2, tm, HEAD_PAD), lambda b, j: (b, 0, j, 0)),
                   pl.BlockSpec((1, tm, GLA_IN_W), row),
                   pl.BlockSpec((1, tm, POOL_WIDTH), row),
                   pl.BlockSpec((1, tm, d), row)],
        out_shape=[head_shape, head_shape,
                   jax.ShapeDtypeStruct((batch, MLA_HEADS // 2, n, HEAD_PAD), BF16),
                   jax.ShapeDtypeStruct((batch, n, GLA_IN_W), BF16),
                   jax.ShapeDtypeStruct((batch, n, POOL_WIDTH), BF16),
                   jax.ShapeDtypeStruct((batch, n, d), BF16)],
        compiler_params=pltpu.CompilerParams(dimension_semantics=("parallel", "parallel"),
                                             vmem_limit_bytes=VMEM_LIMIT),
        name="inproj",
    )(x, ss, lw['pre_g'], lw['w_in'], lw['q_norm'], lw['w_uq'], lw['kv_norm'], lw['w_ukv'], cq, sq, ck, sk)


def _attn_kernel(q_ref, *refs, n_kv):
    kv_refs, o_ref = refs[:2 * n_kv], refs[2 * n_kv]

    def scores(h):
        q = q_ref[0, h]
        return [_dot_nt(q, kv_refs[2 * i][0, h]) for i in range(n_kv)]

    def weighted_values(h, s):
        m = functools.reduce(jnp.maximum, [jnp.max(t, axis=-1, keepdims=True) for t in s])
        e = [jnp.exp2(t - m) for t in s]
        l = functools.reduce(jnp.add, [jnp.sum(t, axis=-1, keepdims=True) for t in e])
        pv = functools.reduce(jnp.add,
                              [_dot(e[i].astype(BF16), kv_refs[2 * i + 1][0, h // 2]) for i in range(n_kv)])
        return pv * (1.0 / l)

    pending = [scores(h) for h in range(SCORE_LOOKAHEAD)]
    even_lanes = lax.broadcasted_iota(jnp.int32, (q_ref.shape[2], HEAD_PAD), 1) < MLA_V
    even = None
    for h in range(MLA_HEADS):
        if h + SCORE_LOOKAHEAD < MLA_HEADS:
            pending.append(scores(h + SCORE_LOOKAHEAD))
        t = weighted_values(h, pending.pop(0))
        if h % 2 == 0:
            even = t
        else:
            pair = jnp.where(even_lanes, even, t)
            o_ref[0, :, (h // 2) * HEAD_PAD:(h // 2 + 1) * HEAD_PAD] = pair.astype(o_ref.dtype)


def _attention(q, kvs):
    batch, heads, n, hp = q.shape
    tq = min(ATTN_TILE, n)
    assert n % tq == 0
    in_specs = [pl.BlockSpec((1, heads, tq, hp), lambda b, j: (b, 0, j, 0))]
    args = [q]
    for k, v in kvs:
        in_specs += [pl.BlockSpec((1, heads, k.shape[2], hp), lambda b, j: (b, 0, 0, 0)),
                     pl.BlockSpec((1, heads // 2, v.shape[2], hp), lambda b, j: (b, 0, 0, 0))]
        args += [k, v]
    return pl.pallas_call(
        functools.partial(_attn_kernel, n_kv=len(kvs)),
        grid=(batch, n // tq),
        in_specs=in_specs,
        out_specs=pl.BlockSpec((1, tq, MLA_WIDTH), lambda b, j: (b, j, 0)),
        out_shape=jax.ShapeDtypeStruct((batch, n, MLA_WIDTH), BF16),
        compiler_params=pltpu.CompilerParams(dimension_semantics=("parallel", "arbitrary"),
                                             vmem_limit_bytes=VMEM_LIMIT),
        name="attention",
    )(*args)


def _gla_kernel(*refs, bb, has_init, with_out):
    g_refs = refs[0:2]
    wg_refs = (refs[2], refs[4])
    bg_refs = (refs[3], refs[5])
    pos = 6
    init_ref = None
    if has_init:
        init_ref = refs[pos]
        pos += 1
    o_refs = None
    if with_out:
        o_refs = (refs[pos], refs[pos + 1])
        pos += 2
    st_ref = refs[pos]

    @pl.when(pl.program_id(1) == 0)
    def _():
        if has_init:
            st_ref[...] = init_ref[...]
        else:
            st_ref[...] = jnp.zeros_like(st_ref)

    c = g_refs[0].shape[1]
    row = lax.broadcasted_iota(jnp.int32, (c, c), 0)
    col = lax.broadcasted_iota(jnp.int32, (c, c), 1)
    tri = (col <= row, col >= row)
    ones_tri = [jnp.where(t, 1.0, 0.0).astype(BF16) for t in tri]
    tri2 = [jnp.concatenate([t, t], axis=1) for t in tri]
    first = lax.broadcasted_iota(jnp.int32, (c, 2 * GLA_DK), 1) < GLA_DK
    first_st = lax.broadcasted_iota(jnp.int32, (GLA_DV, 2 * GLA_DK), 1) < GLA_DK
    k_t, dec, q_inter, q_intra = {}, {}, {}, {}

    def prepare(samples):
        for d in range(2):
            gate_in = jnp.concatenate([g_refs[d][i, :, 2 * GLA_KW + GLA_WIDTH:] for i in samples], axis=0)
            pre = _dot(gate_in, wg_refs[d][...]) + bg_refs[d][...]
            la = (jnp.minimum(pre, 0.0) - jnp.log1p(jnp.exp(-jnp.abs(pre)))) * (1.0 / GLA_TAU)
            la = jnp.concatenate([la[n * c:(n + 1) * c] for n in range(len(samples))], axis=1)
            hi = la.astype(BF16)
            r1 = la - hi.astype(F32)
            mid = r1.astype(BF16)
            lo = (r1 - mid.astype(F32)).astype(BF16)
            bsum = _dot(ones_tri[d], hi) + _dot(ones_tri[d], mid) + _dot(ones_tri[d], lo)
            for n, i in enumerate(samples):
                b = bsum[:, n * GLA_KW:(n + 1) * GLA_KW]
                b_tot = b[0:1] if d == 1 else b[c - 1:c]
                k_t[i, d] = (g_refs[d][i, :, GLA_KW:2 * GLA_KW].astype(F32) * jnp.exp(b_tot - b)).astype(BF16)
                dec[i, d] = jnp.exp(b_tot)
                if with_out:
                    q = g_refs[d][i, :, 0:GLA_KW].astype(F32)
                    q_inter[i, d] = q * jnp.exp(b)
                    q_intra[i, d] = q * jnp.exp(b - b_tot)

    def contract(samples):
        inst = [(i, d) for i in samples for d in range(2)]
        zero_v = jnp.zeros((c, GLA_DV), BF16)
        keys2, att2 = {}, {}
        for i, d in inst:
            for p in range(GLA_PAIRS):
                sl = slice(p * 2 * GLA_DK, (p + 1) * 2 * GLA_DK)
                kt = k_t[i, d][:, sl]
                zero_k = jnp.zeros_like(kt)
                keys2[i, d, p] = jnp.concatenate([jnp.where(first, kt, zero_k), jnp.where(first, zero_k, kt)], axis=0)
                if with_out:
                    a = _dot_nt(q_intra[i, d][:, sl].astype(BF16), keys2[i, d, p])
                    att2[i, d, p] = jnp.where(tri2[d], a, 0.0).astype(BF16)
        for i, d in inst:
            for p in range(GLA_PAIRS):
                sl = slice(p * 2 * GLA_DK, (p + 1) * 2 * GLA_DK)
                st = st_ref[i, d, p]
                lo = 2 * GLA_KW + 2 * p * GLA_DV
                v0 = g_refs[d][i, :, lo:lo + GLA_DV]
                v1 = g_refs[d][i, :, lo + GLA_DV:lo + 2 * GLA_DV]
                if with_out:
                    st2 = jnp.concatenate([jnp.where(first_st, st, 0.0), jnp.where(first_st, 0.0, st)],
                                          axis=0).astype(BF16)
                    v_diag = jnp.concatenate([jnp.concatenate([v0, zero_v], axis=1),
                                              jnp.concatenate([zero_v, v1], axis=1)], axis=0)
                    o = _dot_nt(q_inter[i, d][:, sl].astype(BF16), st2) + _dot(att2[i, d, p], v_diag)
                    o_refs[d][i, :, 2 * p * GLA_DV:(2 * p + 2) * GLA_DV] = o.astype(o_refs[d].dtype)
                upd = _dot_tn(jnp.concatenate([v0, v1], axis=0), keys2[i, d, p])
                st_ref[i, d, p] = st * dec[i, d][:, sl] + upd

    prepare(list(range(bb)))
    contract(list(range(bb)))


def _gla(gla_in, lw, init, with_out):
    batch, n, _ = gla_in.shape
    c = GLA_CHUNK
    nch = n // c
    bb = GLA_BATCH_BLOCK
    const = lambda b, s: (0, 0)
    wspec = pl.BlockSpec((GLA_GATE_PAD, GLA_KW), const)
    bspec = pl.BlockSpec((1, GLA_KW), const)
    fwd = lambda b, s: (b, s, 0)
    bwd = lambda b, s: (b, nch - 1 - s, 0)
    st_shape = (batch, 2, GLA_PAIRS, GLA_DV, 2 * GLA_DK)
    st_spec = pl.BlockSpec((bb,) + st_shape[1:], lambda b, s: (b, 0, 0, 0, 0))
    in_specs = [pl.BlockSpec((bb, c, GLA_IN_W), fwd), pl.BlockSpec((bb, c, GLA_IN_W), bwd),
                wspec, bspec, wspec, bspec]
    args = [gla_in, gla_in, lw['w_af'], lw['b_af'], lw['w_ab'], lw['b_ab']]
    out_specs, out_shape, scratch = [], [], []
    if init is not None:
        in_specs.append(st_spec)
        args.append(init)
    if with_out:
        o_shape = jax.ShapeDtypeStruct((batch, n, GLA_WIDTH), BF16)
        out_specs += [pl.BlockSpec((bb, c, GLA_WIDTH), fwd), pl.BlockSpec((bb, c, GLA_WIDTH), bwd)]
        out_shape += [o_shape, o_shape]
    if init is None:
        out_specs.append(st_spec)
        out_shape.append(jax.ShapeDtypeStruct(st_shape, F32))
    else:
        scratch.append(pltpu.VMEM((bb,) + st_shape[1:], F32))
    return pl.pallas_call(
        functools.partial(_gla_kernel, bb=bb, has_init=init is not None, with_out=with_out),
        grid=(batch // bb, nch),
        in_specs=in_specs,
        out_specs=out_specs,
        out_shape=out_shape,
        scratch_shapes=scratch,
        compiler_params=pltpu.CompilerParams(dimension_semantics=("parallel", "arbitrary"),
                                             vmem_limit_bytes=VMEM_LIMIT),
        name="gla_scan",
    )(*args)


def _merge_kernel(x_ref, ss_ref, xb_ref, att_ref, pc_ref, pp_ref, pn_ref, of_ref, ob_ref,
                  wg_ref, gn_ref, post_ref, pw_ref, ps_ref, wbm_ref, wbp_ref, wbg_ref, wout_ref,
                  o_ref, ubuf):
    tm = x_ref.shape[1]
    j = pl.program_id(1)
    seq_len = tm * pl.num_programs(1)
    x = x_ref[0]
    ss = ss_ref[0]
    xb = xb_ref[0]

    hal = POOL_HALO
    edge = POOL_EDGE
    n_ext = tm + 2 * hal
    no_halo = jnp.zeros_like(pp_ref[0])
    u_ext = jnp.concatenate([jnp.where(j > 0, pp_ref[0], no_halo), pc_ref[0],
                             jnp.where(j < pl.num_programs(1) - 1, pn_ref[0], no_halo)], axis=0)
    before = None
    for lo in range(0, n_ext, POOL_PREFIX_BLOCK):
        rows = min(POOL_PREFIX_BLOCK, n_ext - lo)
        u_blk = u_ext[lo:lo + rows]
        earlier = (lax.broadcasted_iota(jnp.int32, (rows, rows), 1)
                   < lax.broadcasted_iota(jnp.int32, (rows, rows), 0))
        prefix = _dot(jnp.where(earlier, 1.0, 0.0).astype(BF16), u_blk)
        ubuf[lo:lo + rows] = prefix if before is None else prefix + before
        if lo + rows < n_ext:
            total = jnp.sum(u_blk.astype(F32), axis=0, keepdims=True)
            before = total if before is None else before + total
    cur = pc_ref[0].astype(F32)
    r = lax.broadcasted_iota(jnp.int32, (edge, POOL_GROUP), 0)
    pooled = []
    for g, w in enumerate(POOL_WINDOWS):
        lanes = slice(g * POOL_GROUP, (g + 1) * POOL_GROUP)
        acc = ubuf[hal + w // 2:hal + w // 2 + tm, lanes] - ubuf[hal - w // 2:hal - w // 2 + tm, lanes]
        head_pos = j * tm + r
        tail_pos = j * tm + (tm - edge) + r
        head_cnt = (jnp.minimum(head_pos + w // 2, seq_len) - jnp.maximum(head_pos - w // 2, 0)).astype(F32)
        tail_cnt = (jnp.minimum(tail_pos + w // 2, seq_len) - jnp.maximum(tail_pos - w // 2, 0)).astype(F32)
        mean = jnp.concatenate([acc[0:edge] / head_cnt, acc[edge:tm - edge] * (1.0 / w),
                                acc[tm - edge:tm] / tail_cnt], axis=0)
        pooled.append((mean - cur[:, lanes]).astype(BF16))

    branch_gates = _silu_of_half(_dot(xb, wg_ref[:, 0:BRANCH_GATES_W]))
    merge_gates = _dot(xb, wg_ref[:, BRANCH_GATES_W:GATES_W])

    y_mla = (att_ref[0].astype(F32) * branch_gates[:, 0:BRANCH_W]).astype(BF16)
    o = of_ref[0].astype(F32) + ob_ref[0].astype(F32)
    gn = gn_ref[...]
    o = jnp.concatenate([_rmsnorm(o[:, h * GLA_DV:(h + 1) * GLA_DV], gn) for h in range(GLA_HEADS)], axis=-1)
    y_gla = (o * branch_gates[:, 2 * BRANCH_W:3 * BRANCH_W]).astype(BF16)
    t_mla = _dot(y_mla, wbm_ref[...])
    t_gla = _dot(y_gla, wbg_ref[...])
    merged = _twice_sigmoid_of_half(merge_gates[:, 0:D_MODEL]) * t_mla
    merged = merged + _twice_sigmoid_of_half(merge_gates[:, 2 * D_MODEL:3 * D_MODEL]) * t_gla

    mixed = jnp.concatenate(
        [_dot(jnp.concatenate(pooled[2 * i:2 * i + 2], axis=-1), pw_ref[i]) for i in range(2)], axis=-1)
    y_pool = (mixed * ps_ref[...] * branch_gates[:, BRANCH_W:2 * BRANCH_W]).astype(BF16)
    t_pool = _dot(y_pool, wbp_ref[...])

    merged = merged + _twice_sigmoid_of_half(merge_gates[:, D_MODEL:2 * D_MODEL]) * t_pool
    out = _dot(merged.astype(BF16), wout_ref[...])
    o_ref[0] = x + ss[2:3] * _rmsnorm(out, post_ref[...])


def _merge(x, ss, xb, att, pool_x, o_f, o_b, lw, tm):
    batch, n, d = x.shape
    hb = tm // POOL_HALO
    n_halo = n // POOL_HALO
    per_batch_ss = ss.shape[0] == batch
    row = lambda b, j: (b, j, 0)
    const2 = lambda b, j: (0, 0)
    const3 = lambda b, j: (0, 0, 0)
    single = dict(pipeline_mode=pl.Buffered(1))
    return pl.pallas_call(
        _merge_kernel,
        grid=(batch, n // tm),
        in_specs=[pl.BlockSpec((1, tm, d), row),
                  pl.BlockSpec((1, 3, d), (lambda b, j: (b, 0, 0)) if per_batch_ss else (lambda b, j: (0, 0, 0))),
                  pl.BlockSpec((1, tm, d), row),
                  pl.BlockSpec((1, tm, MLA_WIDTH), row),
                  pl.BlockSpec((1, tm, POOL_WIDTH), row),
                  pl.BlockSpec((1, POOL_HALO, POOL_WIDTH), lambda b, j: (b, jnp.maximum(j * hb - 1, 0), 0)),
                  pl.BlockSpec((1, POOL_HALO, POOL_WIDTH),
                               lambda b, j: (b, jnp.minimum((j + 1) * hb, n_halo - 1), 0)),
                  pl.BlockSpec((1, tm, GLA_WIDTH), row),
                  pl.BlockSpec((1, tm, GLA_WIDTH), row),
                  pl.BlockSpec((d, GATES_W), const2, **single),
                  pl.BlockSpec((1, GLA_DV), const2),
                  pl.BlockSpec((1, d), const2),
                  pl.BlockSpec((2, 2 * POOL_GROUP, 2 * POOL_GROUP), const3),
                  pl.BlockSpec((1, POOL_WIDTH), const2),
                  pl.BlockSpec((MLA_WIDTH, d), const2, **single),
                  pl.BlockSpec((POOL_WIDTH, d), const2, **single),
                  pl.BlockSpec((GLA_WIDTH, d), const2, **single),
                  pl.BlockSpec((d, d), const2, **single)],
        out_specs=pl.BlockSpec((1, tm, d), row),
        out_shape=jax.ShapeDtypeStruct((batch, n, d), F32),
        scratch_shapes=[pltpu.VMEM((tm + 2 * POOL_HALO, POOL_WIDTH), F32)],
        compiler_params=pltpu.CompilerParams(dimension_semantics=("parallel", "parallel"),
                                             vmem_limit_bytes=VMEM_LIMIT),
        name="merge",
    )(x, ss, xb, att, pool_x, pool_x, pool_x, o_f, o_b, lw['w_gates'], lw['gla_norm'], lw['post_g'],
      lw['pool_w'], lw['pool_scale'], lw['w_bm'], lw['w_bp'], lw['w_bg'], lw['w_out'])


def _prep_w_in(w):
    offs = [0]
    for s in IN_SIZES:
        offs.append(offs[-1] + s)
    seg = {n: w[:, offs[i]:offs[i + 1]] for i, n in enumerate(IN_NAMES)}
    d = w.shape[0]
    z = lambda n: jnp.zeros((d, n), w.dtype)
    kr_pad = jnp.concatenate([z(MLA_NOPE), seg['mla_kr'], z(HEAD_PAD - MLA_NOPE - MLA_ROPE)], axis=1)
    gate_in = jnp.concatenate([seg['gla_af'], seg['gla_ab'], z(GLA_GATE_PAD - 2 * GLA_GATE_RANK)], axis=1)
    w_heads = jnp.concatenate([seg['mla_q'], seg['mla_kv'], kr_pad,
                               seg['gla_q'] * GLA_Q_SCALE, seg['gla_k'], seg['gla_v'], gate_in, seg['pool_x']],
                              axis=1)
    w_gates = jnp.concatenate([seg['mla_gate'], seg['pool_gate'], seg['gla_gate'], seg['merge']], axis=1) * 0.5
    return w_heads.astype(BF16), w_gates.astype(BF16)


def _prep_w_uq(w):
    r = w.shape[0]
    w3 = w.reshape(r, MLA_HEADS, MLA_NOPE + MLA_ROPE)
    nope, rope = w3[..., :MLA_NOPE], w3[..., MLA_NOPE:]
    pad = jnp.zeros((r, MLA_HEADS, HEAD_PAD - MLA_NOPE - MLA_ROPE), w.dtype)
    return jnp.concatenate([nope, rope, pad], axis=-1).reshape(r, MLA_HEADS * HEAD_PAD).astype(BF16)


def _prep_w_ukv(w):
    r = w.shape[0]
    w3 = w.reshape(r, MLA_HEADS, MLA_NOPE + MLA_V)
    k_nope, v = w3[..., :MLA_NOPE], w3[..., MLA_NOPE:]
    zk = jnp.zeros((r, MLA_HEADS, HEAD_PAD - MLA_NOPE), w.dtype)
    k_pad = jnp.concatenate([k_nope, zk], axis=-1).reshape(r, MLA_HEADS * HEAD_PAD)
    return jnp.concatenate([k_pad, v.reshape(r, MLA_WIDTH)], axis=1).astype(BF16)


def _prep_gate_w(w2, slot):
    full = jnp.zeros((GLA_GATE_PAD, GLA_KW), w2.dtype)
    return full.at[slot * GLA_GATE_RANK:(slot + 1) * GLA_GATE_RANK].set(w2).astype(BF16)


def _prep_pool_w(pw):
    z = jnp.zeros((POOL_GROUP, POOL_GROUP), pw.dtype)
    blocks = [jnp.concatenate([jnp.concatenate([pw[2 * j], z], axis=1),
                               jnp.concatenate([z, pw[2 * j + 1]], axis=1)], axis=0) for j in range(2)]
    return jnp.stack(blocks).astype(BF16)


def _head_tables(cos, sin):
    n = cos.shape[0]
    scale = (MLA_NOPE + MLA_ROPE) ** -0.5 * 1.4426950408889634
    tail = np.zeros((n, HEAD_PAD - MLA_NOPE - MLA_ROPE))
    cq = np.concatenate([np.ones((n, MLA_NOPE)), cos, tail], axis=1) * scale
    sq = np.concatenate([np.zeros((n, MLA_NOPE)), sin, tail], axis=1) * scale
    ck = np.concatenate([np.zeros((n, MLA_NOPE)), cos, tail], axis=1)
    sk = np.concatenate([np.zeros((n, MLA_NOPE)), sin, tail], axis=1)
    return tuple(jnp.asarray(t.astype(np.float32)) for t in (cq, sq, ck, sk))


def _rope_tables(lat_len, ctx_len):
    half = MLA_ROPE // 2
    rows = lat_len // GRID_W
    row = np.repeat(np.arange(rows), GRID_W).astype(np.float64)
    col = np.tile(np.arange(GRID_W), rows).astype(np.float64)
    inv = ROPE_BASE ** (-np.arange(0, half, 2, dtype=np.float64) / half)
    ang_r = row[:, None] * inv
    ang_c = col[:, None] * inv
    ang = np.concatenate([ang_r, ang_r, ang_c, ang_c], axis=-1)
    lat = _head_tables(np.cos(ang), np.sin(ang))
    ctx = _head_tables(np.ones((ctx_len, MLA_ROPE)), np.zeros((ctx_len, MLA_ROPE)))
    return lat, ctx


def _layer_weights(l, pre_norm, post_norm, w_in, mla_q_norm, mla_w_uq, mla_kv_norm, mla_w_ukv, pool_w, pool_scale,
                   gla_af_w2, gla_af_b, gla_ab_w2, gla_ab_b, gla_norm, w_branch_mla, w_branch_pool, w_branch_gla,
                   w_out):
    w_heads, w_gates = _prep_w_in(w_in[l])
    return dict(
        pre_g=pre_norm[l][None], post_g=post_norm[l][None], w_in=w_heads, w_gates=w_gates,
        q_norm=mla_q_norm[l][None], w_uq=_prep_w_uq(mla_w_uq[l]),
        kv_norm=mla_kv_norm[l][None], w_ukv=_prep_w_ukv(mla_w_ukv[l]),
        pool_w=_prep_pool_w(pool_w[l]), pool_scale=pool_scale[l][None],
        w_af=_prep_gate_w(gla_af_w2[l], 0), b_af=gla_af_b[l][None],
        w_ab=_prep_gate_w(gla_ab_w2[l], 1), b_ab=gla_ab_b[l][None],
        gla_norm=gla_norm[l][None],
        w_bm=(0.5 * w_branch_mla[l]).astype(BF16), w_bp=(0.5 * w_branch_pool[l]).astype(BF16),
        w_bg=(0.5 * w_branch_gla[l]).astype(BF16), w_out=w_out[l].astype(BF16))


def kernel(x, c, ctx, c_ctx, mod_w, mod_b, pre_norm, post_norm, w_in, mla_q_norm, mla_w_uq, mla_kv_norm, mla_w_ukv, pool_w, pool_scale, gla_af_w2, gla_af_b, gla_ab_w2, gla_ab_b, gla_norm, w_branch_mla, w_branch_pool, w_branch_gla, w_out):
    batch, lat_len, d = x.shape
    ctx_len = ctx.shape[1]
    depth = mod_w.shape[0]
    ctx_tile = min(ctx_len, LATENT_TILE)
    assert d == D_MODEL and lat_len % LATENT_TILE == 0 and ctx_len % ctx_tile == 0
    assert lat_len % GRID_W == 0 and ctx_len % GLA_CHUNK == 0
    assert batch % GLA_BATCH_BLOCK == 0

    mp = -(-(batch + 1) // 8) * 8
    c_all = jnp.concatenate([c, c_ctx[None], jnp.zeros((mp - batch - 1, d), c.dtype)], axis=0)
    mod_all = _modulation(c_all, mod_w.astype(BF16), mod_b)
    tab_lat, tab_ctx = _rope_tables(lat_len, ctx_len)

    xc = ctx
    for l in range(depth):
        last = l == depth - 1
        lw = _layer_weights(l, pre_norm, post_norm, w_in, mla_q_norm, mla_w_uq, mla_kv_norm, mla_w_ukv, pool_w,
                            pool_scale, gla_af_w2, gla_af_b, gla_ab_w2, gla_ab_b, gla_norm, w_branch_mla,
                            w_branch_pool, w_branch_gla, w_out)
        ss = mod_all[l, :batch].reshape(batch, 3, d)
        ss_c = mod_all[l, batch].reshape(1, 3, d)

        qc, kc, vc, gla_c, pool_c, xb_c = _inproj(xc, ss_c, lw, tab_ctx, ctx_tile)
        q, k, v, gla_x, pool_x, xb = _inproj(x, ss, lw, tab_lat, LATENT_TILE)

        att = _attention(q, [(kc, vc), (k, v)])
        ctx_scan = _gla(gla_c, lw, None, with_out=not last)
        o_f, o_b = _gla(gla_x, lw, ctx_scan[-1], with_out=True)
        if not last:
            att_c = _attention(qc, [(kc, vc)])
            xc = _merge(xc, ss_c, xb_c, att_c, pool_c, ctx_scan[0], ctx_scan[1], lw, ctx_tile)
        x = _merge(x, ss, xb, att, pool_x, o_f, o_b, lw, LATENT_TILE)
    return x
```

```python
import functools

import numpy as np
import jax
import jax.numpy as jnp
from jax import lax
from jax.experimental import pallas as pl
from jax.experimental.pallas import tpu as pltpu

F32 = jnp.float32
BF16 = jnp.bfloat16

D_MODEL = 1024
NORM_EPS = 1e-6
GRID_W = 64
ROPE_BASE = 10000.0

MLA_HEADS = 8
MLA_Q_RANK = 256
MLA_KV_RANK = 128
MLA_NOPE = 64
MLA_ROPE = 32
MLA_V = 64
MLA_WIDTH = MLA_HEADS * MLA_V
HEAD_PAD = 128

POOL_WINDOWS = (2, 4, 8, 16)
POOL_WIDTH = 512
POOL_GROUP = 128
POOL_HALO = 16
POOL_EDGE = max(POOL_WINDOWS) // 2
POOL_PREFIX_BLOCK = 256

GLA_HEADS = 4
GLA_DK = 64
GLA_DV = 128
GLA_KW = GLA_HEADS * GLA_DK
GLA_WIDTH = GLA_HEADS * GLA_DV
GLA_GATE_RANK = 16
GLA_TAU = 16.0
GLA_CHUNK = 128
GLA_PAIRS = GLA_HEADS // 2
GLA_Q_SCALE = GLA_DK ** -0.5

N_BRANCH = 3
IN_SIZES = (MLA_Q_RANK, MLA_KV_RANK, MLA_ROPE, MLA_WIDTH, POOL_WIDTH, POOL_WIDTH,
            GLA_KW, GLA_KW, GLA_WIDTH, GLA_GATE_RANK, GLA_GATE_RANK, GLA_WIDTH, N_BRANCH * D_MODEL)
IN_NAMES = ('mla_q', 'mla_kv', 'mla_kr', 'mla_gate', 'pool_x', 'pool_gate', 'gla_q', 'gla_k', 'gla_v',
            'gla_af', 'gla_ab', 'gla_gate', 'merge')

C_Q = 0
C_KV = C_Q + MLA_Q_RANK
C_KR = C_KV + MLA_KV_RANK
C_GLA = C_KR + HEAD_PAD
GLA_GATE_PAD = 128
GLA_IN_W = 2 * GLA_KW + GLA_WIDTH + GLA_GATE_PAD
C_POOL = C_GLA + GLA_IN_W
C_END = C_POOL + POOL_WIDTH
BRANCH_W = 512
BRANCH_GATES_W = N_BRANCH * BRANCH_W
GATES_W = BRANCH_GATES_W + N_BRANCH * D_MODEL

LATENT_TILE = 512
INPROJ_BLOCK_ROWS = 128
ATTN_TILE = 512
SCORE_LOOKAHEAD = 1
GLA_BATCH_BLOCK = 16
VMEM_LIMIT = 56 * 1024 * 1024


def _silu_of_half(h):
    return h * jnp.tanh(h) + h


def _twice_sigmoid_of_half(h):
    return jnp.tanh(h) + 1.0


def _rmsnorm(x, g):
    return x * lax.rsqrt(jnp.mean(x * x, axis=-1, keepdims=True) + NORM_EPS) * g


def _dot(a, b):
    return jnp.dot(a, b, preferred_element_type=F32)


def _dot_nt(a, b):
    return lax.dot_general(a, b, (((1,), (1,)), ((), ())), preferred_element_type=F32)


def _dot_tn(a, b):
    return lax.dot_general(a, b, (((0,), (0,)), ((), ())), preferred_element_type=F32)


def _modulated(x, ss, pre_g):
    return (_rmsnorm(x, pre_g) * (1.0 + ss[1:2]) + ss[0:1]).astype(BF16)


def _mod_kernel(c_ref, w_ref, b_ref, o_ref):
    c = c_ref[...]
    s = (c / (1.0 + jnp.exp(-c))).astype(BF16)
    o_ref[0] = _dot(s, w_ref[0]) + b_ref[0]


def _modulation(c_all, mod_w, mod_b):
    depth, d, n = mod_w.shape
    mp = c_all.shape[0]
    tn = 1024
    return pl.pallas_call(
        _mod_kernel,
        grid=(depth, n // tn),
        in_specs=[pl.BlockSpec((mp, d), lambda l, j: (0, 0)),
                  pl.BlockSpec((1, d, tn), lambda l, j: (l, 0, j)),
                  pl.BlockSpec((1, 1, tn), lambda l, j: (l, 0, j))],
        out_specs=pl.BlockSpec((1, mp, tn), lambda l, j: (l, 0, j)),
        out_shape=jax.ShapeDtypeStruct((depth, mp, n), F32),
        name="modulation",
    )(c_all, mod_w, mod_b.reshape(depth, 1, n))


def _inproj_kernel(x_ref, ss_ref, pre_ref, w_ref, qn_ref, wuq_ref, kvn_ref, wukv_ref,
                   cq_ref, sq_ref, ck_ref, sk_ref,
                   q_ref, k_ref, v_ref, gla_ref, pool_ref, xb_ref):
    tm = x_ref.shape[1]
    nb = min(tm, INPROJ_BLOCK_ROWS)
    n_blocks = tm // nb
    hw = MLA_HEADS * HEAD_PAD
    quarter = MLA_ROPE // 4
    lane = lax.broadcasted_iota(jnp.int32, (nb, HEAD_PAD), 1)
    takes_negated = lane % (2 * quarter) < quarter

    def rotated(t):
        return jnp.where(takes_negated, -pltpu.roll(t, HEAD_PAD - quarter, axis=1), pltpu.roll(t, quarter, axis=1))

    def project(r0):
        rows = slice(r0, r0 + nb)
        xb = _modulated(x_ref[0, rows, :], ss_ref[0], pre_ref[...])
        xb_ref[0, rows, :] = xb
        zq = _dot(xb, w_ref[:, C_Q:C_KV])
        zkv = _dot(xb, w_ref[:, C_KV:C_KR])
        zkr = _dot(xb, w_ref[:, C_KR:C_GLA])
        gla = _dot(xb, w_ref[:, C_GLA:C_POOL])
        pool = _dot(xb, w_ref[:, C_POOL:C_END])
        qq = _dot(_rmsnorm(zq, qn_ref[...]).astype(BF16), wuq_ref[...])
        kv = _dot(_rmsnorm(zkv, kvn_ref[...]).astype(BF16), wukv_ref[...])
        return zkr, gla, pool, qq, kv

    def emit(r0, zkr, gla, pool, qq, kv):
        rows = slice(r0, r0 + nb)
        gla_ref[0, rows, :] = gla.astype(gla_ref.dtype)
        pool_ref[0, rows, :] = pool.astype(pool_ref.dtype)
        cq, sq = cq_ref[rows, :], sq_ref[rows, :]
        for h in range(MLA_HEADS):
            qh = qq[:, h * HEAD_PAD:(h + 1) * HEAD_PAD]
            q_ref[0, h, rows, :] = (qh * cq + rotated(qh) * sq).astype(q_ref.dtype)
        k_rope = zkr * ck_ref[rows, :] + rotated(zkr) * sk_ref[rows, :]
        for h in range(MLA_HEADS):
            lo = h * HEAD_PAD
            k_ref[0, h, rows, :] = (kv[:, lo:lo + HEAD_PAD] + k_rope).astype(k_ref.dtype)
        for p in range(MLA_HEADS // 2):
            lo = hw + p * HEAD_PAD
            v_ref[0, p, rows, :] = kv[:, lo:lo + HEAD_PAD].astype(v_ref.dtype)

    pending = [project(0)]
    for n in range(n_blocks):
        if n + 1 < n_blocks:
            pending.append(project((n + 1) * nb))
        emit(n * nb, *pending.pop(0))


def _inproj(x, ss, lw, tables, tm):
    batch, n, d = x.shape
    cq, sq, ck, sk = tables
    hw = MLA_HEADS * HEAD_PAD
    per_batch_ss = ss.shape[0] == batch
    const = lambda b, j: (0, 0)
    tab = pl.BlockSpec((tm, HEAD_PAD), lambda b, j: (j, 0))
    head_spec = pl.BlockSpec((1, MLA_HEADS, tm, HEAD_PAD), lambda b, j: (b, 0, j, 0))
    head_shape = jax.ShapeDtypeStruct((batch, MLA_HEADS, n, HEAD_PAD), BF16)
    row = lambda b, j: (b, j, 0)
    return pl.pallas_call(
        _inproj_kernel,
        grid=(batch, n // tm),
        in_specs=[pl.BlockSpec((1, tm, d), row),
                  pl.BlockSpec((1, 3, d), (lambda b, j: (b, 0, 0)) if per_batch_ss else (lambda b, j: (0, 0, 0))),
                  pl.BlockSpec((1, d), const),
                  pl.BlockSpec((d, C_END), const),
                  pl.BlockSpec((1, MLA_Q_RANK), const),
                  pl.BlockSpec((MLA_Q_RANK, hw), const),
                  pl.BlockSpec((1, MLA_KV_RANK), const),
                  pl.BlockSpec((MLA_KV_RANK, hw + MLA_WIDTH), const),
                  tab, tab, tab, tab],
        out_specs=[head_spec, head_spec,
                   pl.BlockSpec((1, MLA_HEADS // 2, tm, HEAD_PAD), lambda b, j: (b, 0, j, 0)),
                   pl.BlockSpec((1, tm, GLA_IN_W), row),
                   pl.BlockSpec((1, tm, POOL_WIDTH), row),
                   pl.BlockSpec((1, tm, d), row)],
        out_shape=[head_shape, head_shape,
                   jax.ShapeDtypeStruct((batch, MLA_HEADS // 2, n, HEAD_PAD), BF16),
                   jax.ShapeDtypeStruct((batch, n, GLA_IN_W), BF16),
                   jax.ShapeDtypeStruct((batch, n, POOL_WIDTH), BF16),
                   jax.ShapeDtypeStruct((batch, n, d), BF16)],
        compiler_params=pltpu.CompilerParams(dimension_semantics=("parallel", "parallel"),
                                             vmem_limit_bytes=VMEM_LIMIT),
        name="inproj",
    )(x, ss, lw['pre_g'], lw['w_in'], lw['q_norm'], lw['w_uq'], lw['kv_norm'], lw['w_ukv'], cq, sq, ck, sk)


def _attn_kernel(q_ref, *refs, n_kv):
    kv_refs, o_ref = refs[:2 * n_kv], refs[2 * n_kv]

    def scores(h):
        q = q_ref[0, h]
        return [_dot_nt(q, kv_refs[2 * i][0, h]) for i in range(n_kv)]

    def weighted_values(h, s):
        m = functools.reduce(jnp.maximum, [jnp.max(t, axis=-1, keepdims=True) for t in s])
        e = [jnp.exp2(t - m) for t in s]
        l = functools.reduce(jnp.add, [jnp.sum(t, axis=-1, keepdims=True) for t in e])
        pv = functools.reduce(jnp.add,
                              [_dot(e[i].astype(BF16), kv_refs[2 * i + 1][0, h // 2]) for i in range(n_kv)])
        return pv * (1.0 / l)

    pending = [scores(h) for h in range(SCORE_LOOKAHEAD)]
    even_lanes = lax.broadcasted_iota(jnp.int32, (q_ref.shape[2], HEAD_PAD), 1) < MLA_V
    even = None
    for h in range(MLA_HEADS):
        if h + SCORE_LOOKAHEAD < MLA_HEADS:
            pending.append(scores(h + SCORE_LOOKAHEAD))
        t = weighted_values(h, pending.pop(0))
        if h % 2 == 0:
            even = t
        else:
            pair = jnp.where(even_lanes, even, t)
            o_ref[0, :, (h // 2) * HEAD_PAD:(h // 2 + 1) * HEAD_PAD] = pair.astype(o_ref.dtype)


def _attention(q, kvs):
    batch, heads, n, hp = q.shape
    tq = min(ATTN_TILE, n)
    assert n % tq == 0
    in_specs = [pl.BlockSpec((1, heads, tq, hp), lambda b, j: (b, 0, j, 0))]
    args = [q]
    for k, v in kvs:
        in_specs += [pl.BlockSpec((1, heads, k.shape[2], hp), lambda b, j: (b, 0, 0, 0)),
                     pl.BlockSpec((1, heads // 2, v.shape[2], hp), lambda b, j: (b, 0, 0, 0))]
        args += [k, v]
    return pl.pallas_call(
        functools.partial(_attn_kernel, n_kv=len(kvs)),
        grid=(batch, n // tq),
        in_specs=in_specs,
        out_specs=pl.BlockSpec((1, tq, MLA_WIDTH), lambda b, j: (b, j, 0)),
        out_shape=jax.ShapeDtypeStruct((batch, n, MLA_WIDTH), BF16),
        compiler_params=pltpu.CompilerParams(dimension_semantics=("parallel", "arbitrary"),
                                             vmem_limit_bytes=VMEM_LIMIT),
        name="attention",
    )(*args)


def _gla_kernel(*refs, bb, has_init, with_out):
    g_refs = refs[0:2]
    wg_refs = (refs[2], refs[4])
    bg_refs = (refs[3], refs[5])
    pos = 6
    init_ref = None
    if has_init:
        init_ref = refs[pos]
        pos += 1
    o_refs = None
    if with_out:
        o_refs = (refs[pos], refs[pos + 1])
        pos += 2
    st_ref = refs[pos]

    @pl.when(pl.program_id(1) == 0)
    def _():
        if has_init:
            st_ref[...] = init_ref[...]
        else:
            st_ref[...] = jnp.zeros_like(st_ref)

    c = g_refs[0].shape[1]
    row = lax.broadcasted_iota(jnp.int32, (c, c), 0)
    col = lax.broadcasted_iota(jnp.int32, (c, c), 1)
    tri = (col <= row, col >= row)
    ones_tri = [jnp.where(t, 1.0, 0.0).astype(BF16) for t in tri]
    tri2 = [jnp.concatenate([t, t], axis=1) for t in tri]
    first = lax.broadcasted_iota(jnp.int32, (c, 2 * GLA_DK), 1) < GLA_DK
    first_st = lax.broadcasted_iota(jnp.int32, (GLA_DV, 2 * GLA_DK), 1) < GLA_DK
    k_t, dec, q_inter, q_intra = {}, {}, {}, {}

    def prepare(samples):
        for d in range(2):
            gate_in = jnp.concatenate([g_refs[d][i, :, 2 * GLA_KW + GLA_WIDTH:] for i in samples], axis=0)
            pre = _dot(gate_in, wg_refs[d][...]) + bg_refs[d][...]
            la = (jnp.minimum(pre, 0.0) - jnp.log1p(jnp.exp(-jnp.abs(pre)))) * (1.0 / GLA_TAU)
            la = jnp.concatenate([la[n * c:(n + 1) * c] for n in range(len(samples))], axis=1)
            hi = la.astype(BF16)
            r1 = la - hi.astype(F32)
            mid = r1.astype(BF16)
            lo = (r1 - mid.astype(F32)).astype(BF16)
            bsum = _dot(ones_tri[d], hi) + _dot(ones_tri[d], mid) + _dot(ones_tri[d], lo)
            for n, i in enumerate(samples):
                b = bsum[:, n * GLA_KW:(n + 1) * GLA_KW]
                b_tot = b[0:1] if d == 1 else b[c - 1:c]
                k_t[i, d] = (g_refs[d][i, :, GLA_KW:2 * GLA_KW].astype(F32) * jnp.exp(b_tot - b)).astype(BF16)
                dec[i, d] = jnp.exp(b_tot)
                if with_out:
                    q = g_refs[d][i, :, 0:GLA_KW].astype(F32)
                    q_inter[i, d] = q * jnp.exp(b)
                    q_intra[i, d] = q * jnp.exp(b - b_tot)

    def contract(samples):
        inst = [(i, d) for i in samples for d in range(2)]
        zero_v = jnp.zeros((c, GLA_DV), BF16)
        keys2, att2 = {}, {}
        for i, d in inst:
            for p in range(GLA_PAIRS):
                sl = slice(p * 2 * GLA_DK, (p + 1) * 2 * GLA_DK)
                kt = k_t[i, d][:, sl]
                zero_k = jnp.zeros_like(kt)
                keys2[i, d, p] = jnp.concatenate([jnp.where(first, kt, zero_k), jnp.where(first, zero_k, kt)], axis=0)
                if with_out:
                    a = _dot_nt(q_intra[i, d][:, sl].astype(BF16), keys2[i, d, p])
                    att2[i, d, p] = jnp.where(tri2[d], a, 0.0).astype(BF16)
        for i, d in inst:
            for p in range(GLA_PAIRS):
                sl = slice(p * 2 * GLA_DK, (p + 1) * 2 * GLA_DK)
                st = st_ref[i, d, p]
                lo = 2 * GLA_KW + 2 * p * GLA_DV
                v0 = g_refs[d][i, :, lo:lo + GLA_DV]
                v1 = g_refs[d][i, :, lo + GLA_DV:lo + 2 * GLA_DV]
                if with_out:
                    st2 = jnp.concatenate([jnp.where(first_st, st, 0.0), jnp.where(first_st, 0.0, st)],
                                          axis=0).astype(BF16)
                    v_diag = jnp.concatenate([jnp.concatenate([v0, zero_v], axis=1),
                                              jnp.concatenate([zero_v, v1], axis=1)], axis=0)
                    o = _dot_nt(q_inter[i, d][:, sl].astype(BF16), st2) + _dot(att2[i, d, p], v_diag)
                    o_refs[d][i, :, 2 * p * GLA_DV:(2 * p + 2) * GLA_DV] = o.astype(o_refs[d].dtype)
                upd = _dot_tn(jnp.concatenate([v0, v1], axis=0), keys2[i, d, p])
                st_ref[i, d, p] = st * dec[i, d][:, sl] + upd

    prepare(list(range(bb)))
    contract(list(range(bb)))


def _gla(gla_in, lw, init, with_out):
    batch, n, _ = gla_in.shape
    c = GLA_CHUNK
    nch = n // c
    bb = GLA_BATCH_BLOCK
    const = lambda b, s: (0, 0)
    wspec = pl.BlockSpec((GLA_GATE_PAD, GLA_KW), const)
    bspec = pl.BlockSpec((1, GLA_KW), const)
    fwd = lambda b, s: (b, s, 0)
    bwd = lambda b, s: (b, nch - 1 - s, 0)
    st_shape = (batch, 2, GLA_PAIRS, GLA_DV, 2 * GLA_DK)
    st_spec = pl.BlockSpec((bb,) + st_shape[1:], lambda b, s: (b, 0, 0, 0, 0))
    in_specs = [pl.BlockSpec((bb, c, GLA_IN_W), fwd), pl.BlockSpec((bb, c, GLA_IN_W), bwd),
                wspec, bspec, wspec, bspec]
    args = [gla_in, gla_in, lw['w_af'], lw['b_af'], lw['w_ab'], lw['b_ab']]
    out_specs, out_shape, scratch = [], [], []
    if init is not None:
        in_specs.append(st_spec)
        args.append(init)
    if with_out:
        o_shape = jax.ShapeDtypeStruct((batch, n, GLA_WIDTH), BF16)
        out_specs += [pl.BlockSpec((bb, c, GLA_WIDTH), fwd), pl.BlockSpec((bb, c, GLA_WIDTH), bwd)]
        out_shape += [o_shape, o_shape]
    if init is None:
        out_specs.append(st_spec)
        out_shape.append(jax.ShapeDtypeStruct(st_shape, F32))
    else:
        scratch.append(pltpu.VMEM((bb,) + st_shape[1:], F32))
    return pl.pallas_call(
        functools.partial(_gla_kernel, bb=bb, has_init=init is not None, with_out=with_out),
        grid=(batch // bb, nch),
        in_specs=in_specs,
        out_specs=out_specs,
        out_shape=out_shape,
        scratch_shapes=scratch,
        compiler_params=pltpu.CompilerParams(dimension_semantics=("parallel", "arbitrary"),
                                             vmem_limit_bytes=VMEM_LIMIT),
        name="gla_scan",
    )(*args)


def _merge_kernel(x_ref, ss_ref, xb_ref, att_ref, pc_ref, pp_ref, pn_ref, of_ref, ob_ref,
                  wg_ref, gn_ref, post_ref, pw_ref, ps_ref, wbm_ref, wbp_ref, wbg_ref, wout_ref,
                  o_ref, ubuf):
    tm = x_ref.shape[1]
    j = pl.program_id(1)
    seq_len = tm * pl.num_programs(1)
    x = x_ref[0]
    ss = ss_ref[0]
    xb = xb_ref[0]

    hal = POOL_HALO
    edge = POOL_EDGE
    n_ext = tm + 2 * hal
    no_halo = jnp.zeros_like(pp_ref[0])
    u_ext = jnp.concatenate([jnp.where(j > 0, pp_ref[0], no_halo), pc_ref[0],
                             jnp.where(j < pl.num_programs(1) - 1, pn_ref[0], no_halo)], axis=0)
    before = None
    for lo in range(0, n_ext, POOL_PREFIX_BLOCK):
        rows = min(POOL_PREFIX_BLOCK, n_ext - lo)
        u_blk = u_ext[lo:lo + rows]
        earlier = (lax.broadcasted_iota(jnp.int32, (rows, rows), 1)
                   < lax.broadcasted_iota(jnp.int32, (rows, rows), 0))
        prefix = _dot(jnp.where(earlier, 1.0, 0.0).astype(BF16), u_blk)
        ubuf[lo:lo + rows] = prefix if before is None else prefix + before
        if lo + rows < n_ext:
            total = jnp.sum(u_blk.astype(F32), axis=0, keepdims=True)
            before = total if before is None else before + total
    cur = pc_ref[0].astype(F32)
    r = lax.broadcasted_iota(jnp.int32, (edge, POOL_GROUP), 0)
    pooled = []
    for g, w in enumerate(POOL_WINDOWS):
        lanes = slice(g * POOL_GROUP, (g + 1) * POOL_GROUP)
        acc = ubuf[hal + w // 2:hal + w // 2 + tm, lanes] - ubuf[hal - w // 2:hal - w // 2 + tm, lanes]
        head_pos = j * tm + r
        tail_pos = j * tm + (tm - edge) + r
        head_cnt = (jnp.minimum(head_pos + w // 2, seq_len) - jnp.maximum(head_pos - w // 2, 0)).astype(F32)
        tail_cnt = (jnp.minimum(tail_pos + w // 2, seq_len) - jnp.maximum(tail_pos - w // 2, 0)).astype(F32)
        mean = jnp.concatenate([acc[0:edge] / head_cnt, acc[edge:tm - edge] * (1.0 / w),
                                acc[tm - edge:tm] / tail_cnt], axis=0)
        pooled.append((mean - cur[:, lanes]).astype(BF16))

    branch_gates = _silu_of_half(_dot(xb, wg_ref[:, 0:BRANCH_GATES_W]))
    merge_gates = _dot(xb, wg_ref[:, BRANCH_GATES_W:GATES_W])

    y_mla = (att_ref[0].astype(F32) * branch_gates[:, 0:BRANCH_W]).astype(BF16)
    o = of_ref[0].astype(F32) + ob_ref[0].astype(F32)
    gn = gn_ref[...]
    o = jnp.concatenate([_rmsnorm(o[:, h * GLA_DV:(h + 1) * GLA_DV], gn) for h in range(GLA_HEADS)], axis=-1)
    y_gla = (o * branch_gates[:, 2 * BRANCH_W:3 * BRANCH_W]).astype(BF16)
    t_mla = _dot(y_mla, wbm_ref[...])
    t_gla = _dot(y_gla, wbg_ref[...])
    merged = _twice_sigmoid_of_half(merge_gates[:, 0:D_MODEL]) * t_mla
    merged = merged + _twice_sigmoid_of_half(merge_gates[:, 2 * D_MODEL:3 * D_MODEL]) * t_gla

    mixed = jnp.concatenate(
        [_dot(jnp.concatenate(pooled[2 * i:2 * i + 2], axis=-1), pw_ref[i]) for i in range(2)], axis=-1)
    y_pool = (mixed * ps_ref[...] * branch_gates[:, BRANCH_W:2 * BRANCH_W]).astype(BF16)
    t_pool = _dot(y_pool, wbp_ref[...])

    merged = merged + _twice_sigmoid_of_half(merge_gates[:, D_MODEL:2 * D_MODEL]) * t_pool
    out = _dot(merged.astype(BF16), wout_ref[...])
    o_ref[0] = x + ss[2:3] * _rmsnorm(out, post_ref[...])


def _merge(x, ss, xb, att, pool_x, o_f, o_b, lw, tm):
    batch, n, d = x.shape
    hb = tm // POOL_HALO
    n_halo = n // POOL_HALO
    per_batch_ss = ss.shape[0] == batch
    row = lambda b, j: (b, j, 0)
    const2 = lambda b, j: (0, 0)
    const3 = lambda b, j: (0, 0, 0)
    single = dict(pipeline_mode=pl.Buffered(1))
    return pl.pallas_call(
        _merge_kernel,
        grid=(batch, n // tm),
        in_specs=[pl.BlockSpec((1, tm, d), row),
                  pl.BlockSpec((1, 3, d), (lambda b, j: (b, 0, 0)) if per_batch_ss else (lambda b, j: (0, 0, 0))),
                  pl.BlockSpec((1, tm, d), row),
                  pl.BlockSpec((1, tm, MLA_WIDTH), row),
                  pl.BlockSpec((1, tm, POOL_WIDTH), row),
                  pl.BlockSpec((1, POOL_HALO, POOL_WIDTH), lambda b, j: (b, jnp.maximum(j * hb - 1, 0), 0)),
                  pl.BlockSpec((1, POOL_HALO, POOL_WIDTH),
                               lambda b, j: (b, jnp.minimum((j + 1) * hb, n_halo - 1), 0)),
                  pl.BlockSpec((1, tm, GLA_WIDTH), row),
                  pl.BlockSpec((1, tm, GLA_WIDTH), row),
                  pl.BlockSpec((d, GATES_W), const2, **single),
                  pl.BlockSpec((1, GLA_DV), const2),
                  pl.BlockSpec((1, d), const2),
                  pl.BlockSpec((2, 2 * POOL_GROUP, 2 * POOL_GROUP), const3),
                  pl.BlockSpec((1, POOL_WIDTH), const2),
                  pl.BlockSpec((MLA_WIDTH, d), const2, **single),
                  pl.BlockSpec((POOL_WIDTH, d), const2, **single),
                  pl.BlockSpec((GLA_WIDTH, d), const2, **single),
                  pl.BlockSpec((d, d), const2, **single)],
        out_specs=pl.BlockSpec((1, tm, d), row),
        out_shape=jax.ShapeDtypeStruct((batch, n, d), F32),
        scratch_shapes=[pltpu.VMEM((tm + 2 * POOL_HALO, POOL_WIDTH), F32)],
        compiler_params=pltpu.CompilerParams(dimension_semantics=("parallel", "parallel"),
                                             vmem_limit_bytes=VMEM_LIMIT),
        name="merge",
    )(x, ss, xb, att, pool_x, pool_x, pool_x, o_f, o_b, lw['w_gates'], lw['gla_norm'], lw['post_g'],
      lw['pool_w'], lw['pool_scale'], lw['w_bm'], lw['w_bp'], lw['w_bg'], lw['w_out'])


def _prep_w_in(w):
    offs = [0]
    for s in IN_SIZES:
        offs.append(offs[-1] + s)
    seg = {n: w[:, offs[i]:offs[i + 1]] for i, n in enumerate(IN_NAMES)}
    d = w.shape[0]
    z = lambda n: jnp.zeros((d, n), w.dtype)
    kr_pad = jnp.concatenate([z(MLA_NOPE), seg['mla_kr'], z(HEAD_PAD - MLA_NOPE - MLA_ROPE)], axis=1)
    gate_in = jnp.concatenate([seg['gla_af'], seg['gla_ab'], z(GLA_GATE_PAD - 2 * GLA_GATE_RANK)], axis=1)
    w_heads = jnp.concatenate([seg['mla_q'], seg['mla_kv'], kr_pad,
                               seg['gla_q'] * GLA_Q_SCALE, seg['gla_k'], seg['gla_v'], gate_in, seg['pool_x']],
                              axis=1)
    w_gates = jnp.concatenate([seg['mla_gate'], seg['pool_gate'], seg['gla_gate'], seg['merge']], axis=1) * 0.5
    return w_heads.astype(BF16), w_gates.astype(BF16)


def _prep_w_uq(w):
    r = w.shape[0]
    w3 = w.reshape(r, MLA_HEADS, MLA_NOPE + MLA_ROPE)
    nope, rope = w3[..., :MLA_NOPE], w3[..., MLA_NOPE:]
    pad = jnp.zeros((r, MLA_HEADS, HEAD_PAD - MLA_NOPE - MLA_ROPE), w.dtype)
    return jnp.concatenate([nope, rope, pad], axis=-1).reshape(r, MLA_HEADS * HEAD_PAD).astype(BF16)


def _prep_w_ukv(w):
    r = w.shape[0]
    w3 = w.reshape(r, MLA_HEADS, MLA_NOPE + MLA_V)
    k_nope, v = w3[..., :MLA_NOPE], w3[..., MLA_NOPE:]
    zk = jnp.zeros((r, MLA_HEADS, HEAD_PAD - MLA_NOPE), w.dtype)
    k_pad = jnp.concatenate([k_nope, zk], axis=-1).reshape(r, MLA_HEADS * HEAD_PAD)
    return jnp.concatenate([k_pad, v.reshape(r, MLA_WIDTH)], axis=1).astype(BF16)


def _prep_gate_w(w2, slot):
    full = jnp.zeros((GLA_GATE_PAD, GLA_KW), w2.dtype)
    return full.at[slot * GLA_GATE_RANK:(slot + 1) * GLA_GATE_RANK].set(w2).astype(BF16)


def _prep_pool_w(pw):
    z = jnp.zeros((POOL_GROUP, POOL_GROUP), pw.dtype)
    blocks = [jnp.concatenate([jnp.concatenate([pw[2 * j], z], axis=1),
                               jnp.concatenate([z, pw[2 * j + 1]], axis=1)], axis=0) for j in range(2)]
    return jnp.stack(blocks).astype(BF16)


def _head_tables(cos, sin):
    n = cos.shape[0]
    scale = (MLA_NOPE + MLA_ROPE) ** -0.5 * 1.4426950408889634
    tail = np.zeros((n, HEAD_PAD - MLA_NOPE - MLA_ROPE))
    cq = np.concatenate([np.ones((n, MLA_NOPE)), cos, tail], axis=1) * scale
    sq = np.concatenate([np.zeros((n, MLA_NOPE)), sin, tail], axis=1) * scale
    ck = np.concatenate([np.zeros((n, MLA_NOPE)), cos, tail], axis=1)
    sk = np.concatenate([np.zeros((n, MLA_NOPE)), sin, tail], axis=1)
    return tuple(jnp.asarray(t.astype(np.float32)) for t in (cq, sq, ck, sk))


def _rope_tables(lat_len, ctx_len):
    half = MLA_ROPE // 2
    rows = lat_len // GRID_W
    row = np.repeat(np.arange(rows), GRID_W).astype(np.float64)
    col = np.tile(np.arange(GRID_W), rows).astype(np.float64)
    inv = ROPE_BASE ** (-np.arange(0, half, 2, dtype=np.float64) / half)
    ang_r = row[:, None] * inv
    ang_c = col[:, None] * inv
    ang = np.concatenate([ang_r, ang_r, ang_c, ang_c], axis=-1)
    lat = _head_tables(np.cos(ang), np.sin(ang))
    ctx = _head_tables(np.ones((ctx_len, MLA_ROPE)), np.zeros((ctx_len, MLA_ROPE)))
    return lat, ctx


def _layer_weights(l, pre_norm, post_norm, w_in, mla_q_norm, mla_w_uq, mla_kv_norm, mla_w_ukv, pool_w, pool_scale,
                   gla_af_w2, gla_af_b, gla_ab_w2, gla_ab_b, gla_norm, w_branch_mla, w_branch_pool, w_branch_gla,
                   w_out):
    w_heads, w_gates = _prep_w_in(w_in[l])
    return dict(
        pre_g=pre_norm[l][None], post_g=post_norm[l][None], w_in=w_heads, w_gates=w_gates,
        q_norm=mla_q_norm[l][None], w_uq=_prep_w_uq(mla_w_uq[l]),
        kv_norm=mla_kv_norm[l][None], w_ukv=_prep_w_ukv(mla_w_ukv[l]),
        pool_w=_prep_pool_w(pool_w[l]), pool_scale=pool_scale[l][None],
        w_af=_prep_gate_w(gla_af_w2[l], 0), b_af=gla_af_b[l][None],
        w_ab=_prep_gate_w(gla_ab_w2[l], 1), b_ab=gla_ab_b[l][None],
        gla_norm=gla_norm[l][None],
        w_bm=(0.5 * w_branch_mla[l]).astype(BF16), w_bp=(0.5 * w_branch_pool[l]).astype(BF16),
        w_bg=(0.5 * w_branch_gla[l]).astype(BF16), w_out=w_out[l].astype(BF16))


def kernel(x, c, ctx, c_ctx, mod_w, mod_b, pre_norm, post_norm, w_in, mla_q_norm, mla_w_uq, mla_kv_norm, mla_w_ukv, pool_w, pool_scale, gla_af_w2, gla_af_b, gla_ab_w2, gla_ab_b, gla_norm, w_branch_mla, w_branch_pool, w_branch_gla, w_out):
    batch, lat_len, d = x.shape
    ctx_len = ctx.shape[1]
    depth = mod_w.shape[0]
    ctx_tile = min(ctx_len, LATENT_TILE)
    assert d == D_MODEL and lat_len % LATENT_TILE == 0 and ctx_len % ctx_tile == 0
    assert lat_len % GRID_W == 0 and ctx_len % GLA_CHUNK == 0
    assert batch % GLA_BATCH_BLOCK == 0

    mp = -(-(batch + 1) // 8) * 8
    c_all = jnp.concatenate([c, c_ctx[None], jnp.zeros((mp - batch - 1, d), c.dtype)], axis=0)
    mod_all = _modulation(c_all, mod_w.astype(BF16), mod_b)
    tab_lat, tab_ctx = _rope_tables(lat_len, ctx_len)

    xc = ctx
    for l in range(depth):
        last = l == depth - 1
        lw = _layer_weights(l, pre_norm, post_norm, w_in, mla_q_norm, mla_w_uq, mla_kv_norm, mla_w_ukv, pool_w,
                            pool_scale, gla_af_w2, gla_af_b, gla_ab_w2, gla_ab_b, gla_norm, w_branch_mla,
                            w_branch_pool, w_branch_gla, w_out)
        ss = mod_all[l, :batch].reshape(batch, 3, d)
        ss_c = mod_all[l, batch].reshape(1, 3, d)

        qc, kc, vc, gla_c, pool_c, xb_c = _inproj(xc, ss_c, lw, tab_ctx, ctx_tile)
        q, k, v, gla_x, pool_x, xb = _inproj(x, ss, lw, tab_lat, LATENT_TILE)

        att = _attention(q, [(kc, vc), (k, v)])
        ctx_scan = _gla(gla_c, lw, None, with_out=not last)
        o_f, o_b = _gla(gla_x, lw, ctx_scan[-1], with_out=True)
        if not last:
            att_c = _attention(qc, [(kc, vc)])
            xc = _merge(xc, ss_c, xb_c, att_c, pool_c, ctx_scan[0], ctx_scan[1], lw, ctx_tile)
        x = _merge(x, ss, xb, att, pool_x, o_f, o_b, lw, LATENT_TILE)
    return x
```
